```python
import math
import jax, jax.numpy as jnp
from jax import lax
import numpy as np

D_MODEL = 4096
BATCH = 4
SEQ = 2048
DEPTH = 1
DEC_BATCH = 128
DEC_SEQ = 1
PAST_LEN = 16384
PAGE_SIZE = 128

D_MIX = 2 * D_MODEL
D_CONV = D_MIX // 4
D_SSM = D_MIX - D_CONV
CONV_WIDTH = 31
SSM_HEAD_DIM = 64
SSM_HEADS = D_SSM // SSM_HEAD_DIM
SSM_GROUPS = 8
SSM_STATE = 128
SSM_CONV = 4
SSM_CHUNK = 128
C_XBC = D_SSM + 2 * SSM_GROUPS * SSM_STATE
D_IN_PROJ = 2 * D_CONV + D_SSM + C_XBC + SSM_HEADS
DT_MIN = 0.001
DT_MAX = 0.1
N_MEM = 256
XA_HEADS = 4
XA_HEAD_DIM = D_MODEL // XA_HEADS
D_FF = 4 * D_MODEL
EPS = 1e-5

kernel_name = 'hymba_conformer_ssd_memxattn_step'


def rmsnorm(x, g):
    x32 = x.astype(jnp.float32)
    y = x32 * lax.rsqrt(jnp.mean(jnp.square(x32), axis=-1, keepdims=True) + EPS)
    return (y * g.astype(jnp.float32)).astype(x.dtype)


def layernorm(x, g, b):
    x32 = x.astype(jnp.float32)
    xc = x32 - jnp.mean(x32, axis=-1, keepdims=True)
    y = xc * lax.rsqrt(jnp.mean(jnp.square(xc), axis=-1, keepdims=True) + EPS)
    return (y * g.astype(jnp.float32) + b.astype(jnp.float32)).astype(x.dtype)


def causal_dwconv(x_full, w, b):
    ch = w.shape[1]
    y = lax.conv_general_dilated(x_full, w[:, None, :].astype(x_full.dtype), window_strides=(1,),
                                 padding='VALID', dimension_numbers=('NWC', 'WIO', 'NWC'),
                                 feature_group_count=ch)
    return y + b.astype(y.dtype)


def ssd_chunked(x, dt, a, bm, cm, h0):
    bsz, seqlen = x.shape[0], x.shape[1]
    q = min(SSM_CHUNK, seqlen)
    nc = -(-seqlen // q)
    pad = nc * q - seqlen
    if pad:
        x = jnp.pad(x, ((0, 0), (0, pad), (0, 0), (0, 0)))
        dt = jnp.pad(dt, ((0, 0), (0, pad), (0, 0)))
        bm = jnp.pad(bm, ((0, 0), (0, pad), (0, 0), (0, 0)))
        cm = jnp.pad(cm, ((0, 0), (0, pad), (0, 0), (0, 0)))
    g, r = SSM_GROUPS, SSM_HEADS // SSM_GROUPS
    xc = x.reshape(bsz, nc, q, g, r, SSM_HEAD_DIM)
    dtc = dt.reshape(bsz, nc, q, g, r)
    bc = bm.reshape(bsz, nc, q, g, SSM_STATE)
    cc = cm.reshape(bsz, nc, q, g, SSM_STATE)
    cs = jnp.cumsum(dtc * a.reshape(g, r), axis=2)
    xdt = xc * dtc[..., None]
    seg = cs[:, :, :, None] - cs[:, :, None, :]
    causal = jnp.tril(jnp.ones((q, q), dtype=bool))[:, :, None, None]
    lmat = jnp.exp(jnp.where(causal, seg, -jnp.inf))
    cb = jnp.einsum('bcqgn,bcsgn->bcqsg', cc, bc)
    y_diag = jnp.einsum('bcqsgr,bcsgrp->bcqgrp', cb[..., None] * lmat, xdt)
    decay = jnp.exp(cs[:, :, -1:] - cs)
    st = jnp.einsum('bcsgn,bcsgrp->bcgrpn', bc, xdt * decay[..., None])
    chunk_decay = jnp.exp(cs[:, :, -1])

    def step(h, inp):
        s_c, d_c = inp
        return h * d_c[..., None, None] + s_c, h

    h_last, h_in = lax.scan(step, h0.reshape(bsz, g, r, SSM_HEAD_DIM, SSM_STATE),
                            (jnp.moveaxis(st, 1, 0), jnp.moveaxis(chunk_decay, 1, 0)))
    h_in = jnp.moveaxis(h_in, 0, 1)
    y_off = jnp.einsum('bcqgn,bcgrpn->bcqgrp', cc, h_in) * jnp.exp(cs)[..., None]
    y = (y_diag + y_off).reshape(bsz, nc * q, SSM_HEADS, SSM_HEAD_DIM)[:, :seqlen]
    return y, h_last.reshape(bsz, SSM_HEADS, SSM_HEAD_DIM, SSM_STATE)


def parallel_mixer(h, conv_hist, ssm_hist, ssm_state, w_in, conv_w, conv_b, ln_g, ln_b,
                   ssm_conv_w, ssm_conv_b, dt_bias, a_log, d_skip, ssm_norm_g, w_out):
    f32 = jnp.float32
    bsz, seqlen, _ = h.shape
    proj = h @ w_in
    glu_a, glu_g, z, xbc, dt_raw = jnp.split(
        proj, [D_CONV, 2 * D_CONV, 2 * D_CONV + D_SSM, 2 * D_CONV + D_SSM + C_XBC], axis=-1)
    u = glu_a * jax.nn.sigmoid(glu_g)
    u_full = jnp.concatenate([conv_hist.astype(u.dtype), u], axis=1)
    new_conv_hist = u_full[:, -(CONV_WIDTH - 1):]
    v = jax.nn.silu(layernorm(causal_dwconv(u_full, conv_w, conv_b), ln_g, ln_b))
    xbc_full = jnp.concatenate([ssm_hist.astype(xbc.dtype), xbc], axis=1)
    new_ssm_hist = xbc_full[:, -(SSM_CONV - 1):]
    xbc_c = jax.nn.silu(causal_dwconv(xbc_full, ssm_conv_w, ssm_conv_b)).astype(f32)
    xs, bm, cm = jnp.split(xbc_c, [D_SSM, D_SSM + SSM_GROUPS * SSM_STATE], axis=-1)
    xs = xs.reshape(bsz, seqlen, SSM_HEADS, SSM_HEAD_DIM)
    dt = jax.nn.softplus(dt_raw.astype(f32) + dt_bias.astype(f32))
    a = -jnp.exp(a_log.astype(f32))
    y, new_state = ssd_chunked(xs, dt, a,
                               bm.reshape(bsz, seqlen, SSM_GROUPS, SSM_STATE),
                               cm.reshape(bsz, seqlen, SSM_GROUPS, SSM_STATE),
                               ssm_state.astype(f32))
    y = y + d_skip.astype(f32)[:, None] * xs
    y = y.reshape(bsz, seqlen, D_SSM) * jax.nn.silu(z.astype(f32))
    yg = y.reshape(bsz, seqlen, SSM_GROUPS, D_SSM // SSM_GROUPS)
    yg = yg * lax.rsqrt(jnp.mean(jnp.square(yg), axis=-1, keepdims=True) + EPS)
    y = yg.reshape(bsz, seqlen, D_SSM) * ssm_norm_g.astype(f32)
    out = jnp.concatenate([v, y.astype(h.dtype)], axis=-1) @ w_out
    return out, new_conv_hist, new_ssm_hist, new_state.astype(h.dtype)


def memory_kv(mem, g_mem, w_k, w_v):
    bsz = mem.shape[0]
    m = rmsnorm(mem, g_mem)
    k = (m @ w_k).reshape(bsz, N_MEM, XA_HEADS, XA_HEAD_DIM)
    v = (m @ w_v).reshape(bsz, N_MEM, XA_HEADS, XA_HEAD_DIM)
    return k, v


def cross_attend(h, mem_k, mem_v, w_q, w_o):
    bsz, seqlen, _ = h.shape
    q = (h @ w_q).reshape(bsz, seqlen, XA_HEADS, XA_HEAD_DIM)
    s = jnp.einsum('blhd,bmhd->bhlm', q, mem_k.astype(q.dtype),
                   preferred_element_type=jnp.float32) * (XA_HEAD_DIM ** -0.5)
    p = jax.nn.softmax(s, axis=-1).astype(h.dtype)
    o = jnp.einsum('bhlm,bmhd->blhd', p, mem_v.astype(h.dtype)).reshape(bsz, seqlen, D_MODEL)
    return o @ w_o


def decoder_layer(x, mem_k, mem_v, conv_hist, ssm_hist, ssm_state,
                  g_mix, w_in, conv_w, conv_b, ln_g, ln_b, ssm_conv_w, ssm_conv_b,
                  dt_bias, a_log, d_skip, ssm_norm_g, w_out, g_xattn, w_q, w_o,
                  g_mlp, w_up, w_down):
    m, new_conv, new_ssm_conv, new_ssm = parallel_mixer(
        rmsnorm(x, g_mix), conv_hist, ssm_hist, ssm_state, w_in, conv_w, conv_b, ln_g, ln_b,
        ssm_conv_w, ssm_conv_b, dt_bias, a_log, d_skip, ssm_norm_g, w_out)
    x = x + m
    x = x + cross_attend(rmsnorm(x, g_xattn), mem_k, mem_v, w_q, w_o)
    x = x + jnp.square(jax.nn.relu(rmsnorm(x, g_mlp) @ w_up)) @ w_down
    return x, new_conv, new_ssm_conv, new_ssm


def setup_inputs(seed: int = 0) -> dict:
    key = jax.random.key(seed)
    ks = jax.random.split(key, 32)
    f32 = jnp.float32
    L = DEPTH

    def nrm(k, shape, scale):
        return jax.random.normal(k, shape, f32) * scale

    def gain(k, shape):
        return 1.0 + 0.02 * jax.random.normal(k, shape, f32)

    u_dt = jax.random.uniform(ks[16], (L, SSM_HEADS), f32)
    dt0 = jnp.exp(u_dt * (math.log(DT_MAX) - math.log(DT_MIN)) + math.log(DT_MIN))
    dt_bias = dt0 + jnp.log(-jnp.expm1(-dt0))
    a_log = jnp.log(jax.random.uniform(ks[17], (L, SSM_HEADS), f32, 1.0, 16.0))
    return {
        'x_prompt': nrm(ks[0], (BATCH, SEQ, D_MODEL), 1.0),
        'x_sample': nrm(ks[1], (DEC_BATCH, DEC_SEQ, D_MODEL), 1.0),
        'mem_prompt': nrm(ks[2], (BATCH, N_MEM, D_MODEL), 1.0),
        'state_conv': nrm(ks[3], (L, DEC_BATCH, CONV_WIDTH - 1, D_CONV), 1.0),
        'state_ssm_conv': nrm(ks[4], (L, DEC_BATCH, SSM_CONV - 1, C_XBC), 1.0),
        'state_ssm': nrm(ks[5], (L, DEC_BATCH, SSM_HEADS, SSM_HEAD_DIM, SSM_STATE), 0.1),
        'cache_mem_k': nrm(ks[6], (L, DEC_BATCH, N_MEM, XA_HEADS, XA_HEAD_DIM), 1.0),
        'cache_mem_v': nrm(ks[7], (L, DEC_BATCH, N_MEM, XA_HEADS, XA_HEAD_DIM), 1.0),
        'g_mix': gain(ks[8], (L, D_MODEL)),
        'w_in': nrm(ks[9], (L, D_MODEL, D_IN_PROJ), D_MODEL ** -0.5),
        'conv_w': nrm(ks[10], (L, CONV_WIDTH, D_CONV), CONV_WIDTH ** -0.5),
        'conv_b': nrm(ks[11], (L, D_CONV), 0.01),
        'ln_g': gain(ks[12], (L, D_CONV)),
        'ln_b': nrm(ks[13], (L, D_CONV), 0.01),
        'ssm_conv_w': nrm(ks[14], (L, SSM_CONV, C_XBC), SSM_CONV ** -0.5),
        'ssm_conv_b': nrm(ks[15], (L, C_XBC), 0.01),
        'dt_bias': dt_bias,
        'a_log': a_log,
        'd_skip': 1.0 + 0.1 * jax.random.normal(ks[18], (L, SSM_HEADS), f32),
        'ssm_norm_g': gain(ks[19], (L, D_SSM)),
        'w_out': nrm(ks[20], (L, D_MIX, D_MODEL), D_MIX ** -0.5),
        'g_xattn': gain(ks[21], (L, D_MODEL)),
        'g_mem': gain(ks[22], (L, D_MODEL)),
        'w_q': nrm(ks[23], (L, D_MODEL, XA_HEADS * XA_HEAD_DIM), D_MODEL ** -0.5),
        'w_k': nrm(ks[24], (L, D_MODEL, XA_HEADS * XA_HEAD_DIM), D_MODEL ** -0.5),
        'w_v': nrm(ks[25], (L, D_MODEL, XA_HEADS * XA_HEAD_DIM), D_MODEL ** -0.5),
        'w_o': nrm(ks[26], (L, XA_HEADS * XA_HEAD_DIM, D_MODEL), (XA_HEADS * XA_HEAD_DIM) ** -0.5),
        'g_mlp': gain(ks[27], (L, D_MODEL)),
        'w_up': nrm(ks[28], (L, D_MODEL, D_FF), D_MODEL ** -0.5),
        'w_down': nrm(ks[29], (L, D_FF, D_MODEL), D_FF ** -0.5),
        'g_final': gain(ks[30], (D_MODEL,)),
    }


def reference(x_prompt, x_sample, mem_prompt, state_conv, state_ssm_conv, state_ssm,
              cache_mem_k, cache_mem_v, g_mix, w_in, conv_w, conv_b, ln_g, ln_b,
              ssm_conv_w, ssm_conv_b, dt_bias, a_log, d_skip, ssm_norm_g, w_out,
              g_xattn, g_mem, w_q, w_k, w_v, w_o, g_mlp, w_up, w_down, g_final):
    dtype = x_prompt.dtype
    hp, hs = x_prompt, x_sample
    conv_p, ssmc_p, ssm_p, mk_p, mv_p = [], [], [], [], []
    conv_s, ssmc_s, ssm_s = [], [], []
    for i in range(DEPTH):
        lw = (g_mix[i], w_in[i], conv_w[i], conv_b[i], ln_g[i], ln_b[i], ssm_conv_w[i],
              ssm_conv_b[i], dt_bias[i], a_log[i], d_skip[i], ssm_norm_g[i], w_out[i],
              g_xattn[i], w_q[i], w_o[i], g_mlp[i], w_up[i], w_down[i])
        k_p, v_p = memory_kv(mem_prompt, g_mem[i], w_k[i], w_v[i])
        hp, c1, c2, c3 = decoder_layer(
            hp, k_p, v_p,
            jnp.zeros((hp.shape[0], CONV_WIDTH - 1, D_CONV), dtype),
            jnp.zeros((hp.shape[0], SSM_CONV - 1, C_XBC), dtype),
            jnp.zeros((hp.shape[0], SSM_HEADS, SSM_HEAD_DIM, SSM_STATE), dtype),
            *lw)
        conv_p.append(c1); ssmc_p.append(c2); ssm_p.append(c3); mk_p.append(k_p); mv_p.append(v_p)
        hs, s1, s2, s3 = decoder_layer(
            hs, cache_mem_k[i], cache_mem_v[i], state_conv[i], state_ssm_conv[i], state_ssm[i], *lw)
        conv_s.append(s1); ssmc_s.append(s2); ssm_s.append(s3)
    y_prompt = rmsnorm(hp, g_final)
    y_sample = rmsnorm(hs, g_final)
    return (y_prompt, y_sample,
            jnp.stack(conv_p), jnp.stack(ssmc_p), jnp.stack(ssm_p), jnp.stack(mk_p), jnp.stack(mv_p),
            jnp.stack(conv_s), jnp.stack(ssmc_s), jnp.stack(ssm_s))
```

```python
import functools

import jax
import jax.numpy as jnp
from jax import lax
from jax.experimental import pallas as pl
from jax.experimental.pallas import tpu as pltpu

F32 = jnp.float32
BF16 = jnp.bfloat16
EPS = 1e-5
LANES = 128
SUBLANES = 8
MXU_DIM = 256
VMEM_CAP = 56 * 1024 * 1024
HIGHEST = lax.Precision.HIGHEST


def _params(sem, vmem_bytes):
    return pltpu.CompilerParams(dimension_semantics=sem,
                                vmem_limit_bytes=int(min(VMEM_CAP, max(vmem_bytes, 16 * 1024 * 1024))))


def _silu(x):
    return x * jax.nn.sigmoid(x)


def _tile(dim, cap, unit):
    if dim <= cap:
        return dim
    best = max(t for t in range(unit, cap + 1, unit) if dim % t == 0)
    return best


def _rmsnorm_body(x_ref, g_ref, o_ref):
    x = x_ref[...]
    ms = jnp.mean(x * x, axis=-1, keepdims=True)
    o_ref[...] = (x * lax.rsqrt(ms + EPS) * g_ref[...]).astype(o_ref.dtype)


def _rmsnorm(x, g, out_dtype, bm=256):
    m, d = x.shape
    bm = min(bm, m)
    return pl.pallas_call(
        _rmsnorm_body,
        grid=(m // bm,),
        in_specs=[pl.BlockSpec((bm, d), lambda i: (i, 0)),
                  pl.BlockSpec((1, d), lambda i: (0, 0))],
        out_specs=pl.BlockSpec((bm, d), lambda i: (i, 0)),
        out_shape=jax.ShapeDtypeStruct((m, d), out_dtype),
        compiler_params=_params(("parallel",), 6 * bm * d * 4),
        name="rmsnorm",
    )(x, g.reshape(1, d).astype(F32))


def _matmul_body(*refs, nk, act, has_res):
    a_ref, w_ref = refs[0], refs[1]
    res_ref = refs[2] if has_res else None
    o_ref = refs[3] if has_res else refs[2]
    acc_ref = refs[-1] if nk > 1 else None

    def finish(acc):
        if act == "relu2":
            acc = jnp.square(jnp.maximum(acc, 0.0))
        if has_res:
            acc = res_ref[...] + acc
        o_ref[...] = acc.astype(o_ref.dtype)

    part = jnp.dot(a_ref[...], w_ref[...], preferred_element_type=F32)
    if nk == 1:
        finish(part)
        return
    k = pl.program_id(2)

    @pl.when(k == 0)
    def _():
        acc_ref[...] = part

    @pl.when(k > 0)
    def _():
        acc_ref[...] += part

    @pl.when(k == nk - 1)
    def _():
        finish(acc_ref[...])


def _matmul(a, w, *, out_dtype, act=None, res=None):
    m, kdim = a.shape
    n = w.shape[1]
    bm = _tile(m, 1024, SUBLANES)
    bk = kdim if kdim <= 4096 else _tile(kdim, 2048, LANES)
    wide = not (res is not None and bk == kdim and bm == 1024)
    bn = _tile(n, 1024 if wide else 512, LANES)
    nk = kdim // bk
    assert m % bm == 0 and n % bn == 0 and kdim % bk == 0
    osz = jnp.dtype(out_dtype).itemsize
    vmem = 2 * (bm * bk * 2 + bk * bn * 2 + bm * bn * osz) + 2 * bm * bn * 4
    in_specs = [pl.BlockSpec((bm, bk), lambda i, j, k: (i, k)),
                pl.BlockSpec((bk, bn), lambda i, j, k: (k, j))]
    args = [a, w]
    if res is not None:
        in_specs.append(pl.BlockSpec((bm, bn), lambda i, j, k: (i, j)))
        args.append(res)
        vmem += 2 * bm * bn * 4
    scratch = [pltpu.VMEM((bm, bn), F32)] if nk > 1 else []
    return pl.pallas_call(
        functools.partial(_matmul_body, nk=nk, act=act, has_res=res is not None),
        grid=(m // bm, n // bn, nk),
        in_specs=in_specs,
        out_specs=pl.BlockSpec((bm, bn), lambda i, j, k: (i, j)),
        out_shape=jax.ShapeDtypeStruct((m, n), out_dtype),
        scratch_shapes=scratch,
        compiler_params=_params(("parallel", "parallel", "arbitrary"), vmem + 4 * 1024 * 1024),
        name="matmul",
    )(*args)


CONV_HALO = 32
CONV_ROWS = 32
CONV_STRIP = 512


def _conv_prompt_body(a_ref, g_ref, w_ref, b_ref, lng_ref, lnb_ref, v_ref, hist_ref, ubuf, cbuf,
                      *, tile, width, nt):
    t = pl.program_id(1)
    ch = ubuf.shape[1]

    @pl.when(t == 0)
    def _():
        ubuf[0:CONV_HALO, :] = jnp.zeros((CONV_HALO, ch), F32)

    ubuf[CONV_HALO:CONV_HALO + tile, :] = a_ref[...] * jax.nn.sigmoid(g_ref[...])
    off0 = CONV_HALO - (width - 1)

    def strip(c, carry):
        col = pl.ds(pl.multiple_of(c * CONV_STRIP, CONV_STRIP), CONV_STRIP)
        for r0 in range(0, tile, CONV_ROWS):
            acc = jnp.broadcast_to(b_ref[:, col], (CONV_ROWS, CONV_STRIP))
            for k in range(width):
                acc = acc + w_ref[k:k + 1, col] * ubuf[pl.ds(r0 + off0 + k, CONV_ROWS), col]
            cbuf[r0:r0 + CONV_ROWS, col] = acc
        return carry

    lax.fori_loop(0, ch // CONV_STRIP, strip, 0)

    y = cbuf[...]
    yc = y - jnp.mean(y, axis=-1, keepdims=True)
    yn = yc * lax.rsqrt(jnp.mean(yc * yc, axis=-1, keepdims=True) + EPS)
    v_ref[...] = _silu(yn * lng_ref[...] + lnb_ref[...]).astype(v_ref.dtype)

    ubuf[0:CONV_HALO, :] = ubuf[tile:tile + CONV_HALO, :]

    @pl.when(t == nt - 1)
    def _():
        hist_ref[0] = ubuf[0:CONV_HALO, :]


def _conv_prompt(glu, conv_w, conv_b, ln_g, ln_b, batch, seqlen):
    width, ch = conv_w.shape
    tile = min(128, seqlen)
    nt = seqlen // tile
    assert width - 1 <= CONV_HALO <= tile and seqlen % tile == 0
    assert ch % CONV_STRIP == 0 and tile % CONV_ROWS == 0
    row = lambda b, t: (b * nt + t, 0)
    vec = pl.BlockSpec((1, ch), lambda b, t: (0, 0))
    return pl.pallas_call(
        functools.partial(_conv_prompt_body, tile=tile, width=width, nt=nt),
        grid=(batch, nt),
        in_specs=[pl.BlockSpec((tile, ch), row),
                  pl.BlockSpec((tile, ch), lambda b, t: (b * nt + t, 1)),
                  pl.BlockSpec((width, ch), lambda b, t: (0, 0)),
                  vec, vec, vec],
        out_specs=[pl.BlockSpec((tile, ch), row),
                   pl.BlockSpec((1, CONV_HALO, ch), lambda b, t: (b, 0, 0))],
        out_shape=[jax.ShapeDtypeStruct((batch * seqlen, ch), BF16),
                   jax.ShapeDtypeStruct((batch, CONV_HALO, ch), F32)],
        scratch_shapes=[pltpu.VMEM((CONV_HALO + tile, ch), F32),
                        pltpu.VMEM((tile, ch), F32)],
        compiler_params=_params(("arbitrary", "arbitrary"), 32 * 1024 * 1024),
        name="conv_prompt",
    )(glu, glu, conv_w, conv_b.reshape(1, ch), ln_g.reshape(1, ch), ln_b.reshape(1, ch))


def _conv_sample_body(a_ref, g_ref, st_ref, w_ref, b_ref, lng_ref, lnb_ref, v_ref, nst_ref, *, width):
    u = a_ref[...] * jax.nn.sigmoid(g_ref[...])
    hist = st_ref[...]
    w = w_ref[...]
    y = jnp.sum(hist * w[None, 0:width - 1, :], axis=1) + u * w[width - 1:width, :] + b_ref[...]
    yc = y - jnp.mean(y, axis=-1, keepdims=True)
    yn = yc * lax.rsqrt(jnp.mean(yc * yc, axis=-1, keepdims=True) + EPS)
    v_ref[...] = _silu(yn * lng_ref[...] + lnb_ref[...]).astype(v_ref.dtype)
    nst_ref[:, 0:width - 2, :] = hist[:, 1:width - 1, :]
    nst_ref[:, width - 2:width - 1, :] = u[:, None, :]


def _conv_sample(glu, state, conv_w, conv_b, ln_g, ln_b):
    width, ch = conv_w.shape
    nb = glu.shape[0]
    bs = SUBLANES
    assert nb % bs == 0
    vec = pl.BlockSpec((1, ch), lambda i: (0, 0))
    return pl.pallas_call(
        functools.partial(_conv_sample_body, width=width),
        grid=(nb // bs,),
        in_specs=[pl.BlockSpec((bs, ch), lambda i: (i, 0)),
                  pl.BlockSpec((bs, ch), lambda i: (i, 1)),
                  pl.BlockSpec((bs, width - 1, ch), lambda i: (i, 0, 0)),
                  pl.BlockSpec((width, ch), lambda i: (0, 0)),
                  vec, vec, vec],
        out_specs=[pl.BlockSpec((bs, ch), lambda i: (i, 0)),
                   pl.BlockSpec((bs, width - 1, ch), lambda i: (i, 0, 0))],
        out_shape=[jax.ShapeDtypeStruct((nb, ch), BF16),
                   jax.ShapeDtypeStruct(state.shape, F32)],
        compiler_params=_params(("parallel",), 32 * 1024 * 1024),
        name="conv_sample",
    )(glu, glu, state, conv_w, conv_b.reshape(1, ch), ln_g.reshape(1, ch), ln_b.reshape(1, ch))


def _head_expand_matrix(heads_per_group, head_dim):
    gw = heads_per_group * head_dim
    return (jnp.arange(LANES)[:, None] == (jnp.arange(gw)[None, :] // head_dim)).astype(F32)


def _pad_heads(v, groups):
    hg = v.shape[0] // groups
    return jnp.pad(v.reshape(groups, hg).astype(F32), ((0, 0), (0, LANES - hg))).reshape(1, groups * LANES)


def _group_rmsnorm_gate(y, xs, z, dskip, normg):
    y = (y + dskip * xs) * _silu(z)
    return y * lax.rsqrt(jnp.mean(y * y, axis=-1, keepdims=True) + EPS) * normg


def _ssd_prompt_body(xs_ref, z_ref, b_ref, c_ref, dt_ref, wx_ref, wb_ref, wc_ref, bx_ref, bb_ref, bc_ref,
                     dtb_ref, alog_ref, dskip_ref, normg_ref, expand_ref,
                     y_ref, st_ref, state, xbuf, bbuf, cbuf, *, q, nc, head_dim, cw):
    c = pl.program_id(2)
    gw = xbuf.shape[1]
    n = bbuf.shape[1]
    hg = gw // head_dim
    halo = SUBLANES

    @pl.when(c == 0)
    def _():
        state[...] = jnp.zeros(state.shape, F32)
        xbuf[0:halo, :] = jnp.zeros((halo, gw), F32)
        bbuf[0:halo, :] = jnp.zeros((halo, n), F32)
        cbuf[0:halo, :] = jnp.zeros((halo, n), F32)

    def causal_conv(raw_ref, buf, w_ref, bias_ref):
        buf[halo:halo + q, :] = raw_ref[...]
        acc = bias_ref[...] + w_ref[0:1, :] * buf[pl.ds(halo - (cw - 1), q), :]
        for k in range(1, cw):
            acc = acc + w_ref[k:k + 1, :] * buf[pl.ds(halo - (cw - 1) + k, q), :]
        buf[0:halo, :] = buf[q:q + halo, :]
        return _silu(acc)

    xs = causal_conv(xs_ref, xbuf, wx_ref, bx_ref)
    bm = causal_conv(b_ref, bbuf, wb_ref, bb_ref)
    cm = causal_conv(c_ref, cbuf, wc_ref, bc_ref)

    dt = jax.nn.softplus(dt_ref[...] + dtb_ref[...])
    dta = dt * (-jnp.exp(alog_ref[...]))
    qi = lax.broadcasted_iota(jnp.int32, (q, q), 0)
    si = lax.broadcasted_iota(jnp.int32, (q, q), 1)
    causal = qi >= si
    cs = jnp.dot(causal.astype(F32), dta, precision=HIGHEST, preferred_element_type=F32)
    cs_t = cs.T
    exp_cs = jnp.exp(cs)
    decay_end = jnp.exp(cs[q - 1:q, :] - cs)
    lane_rep = jnp.dot(jnp.concatenate([dt, exp_cs, decay_end], axis=0), expand_ref[...],
                       precision=HIGHEST, preferred_element_type=F32)
    dt_x, exp_cs_x, decay_x = lane_rep[0:q], lane_rep[q:2 * q], lane_rep[2 * q:3 * q]

    xdt = xs * dt_x
    cm16 = cm.astype(BF16)
    cb = lax.dot_general(cm16, bm.astype(BF16), (((1,), (1,)), ((), ())), preferred_element_type=F32)

    hb = MXU_DIM // head_dim
    slab = hb * head_dim
    ri = lax.broadcasted_iota(jnp.int32, (hb * q, slab), 0) // q
    ci = lax.broadcasted_iota(jnp.int32, (hb * q, slab), 1) // head_dim
    blockdiag = ri == ci
    y_parts = []
    for s in range(hg // hb):
        lhs = []
        for j in range(hb):
            h = s * hb + j
            seg = cs[:, h:h + 1] - cs_t[h:h + 1, :]
            lmat = jnp.where(causal, jnp.exp(jnp.where(causal, seg, 0.0)), 0.0)
            lhs.append((cb * lmat).astype(BF16))
        xslab = xdt[:, s * slab:(s + 1) * slab].astype(BF16)
        rhs = jnp.where(blockdiag, jnp.concatenate([xslab] * hb, axis=0), jnp.zeros((), BF16))
        y_parts.append(jnp.dot(jnp.concatenate(lhs, axis=1), rhs, preferred_element_type=F32))
    y = jnp.concatenate(y_parts, axis=1)

    st = state[...]
    y = y + jnp.dot(cm16, st.astype(BF16), preferred_element_type=F32) * exp_cs_x
    upd = jnp.dot(bm.T.astype(BF16), (xdt * decay_x).astype(BF16), preferred_element_type=F32)
    new_state = st * exp_cs_x[q - 1:q, :] + upd
    state[...] = new_state

    y_ref[...] = _group_rmsnorm_gate(y, xs, z_ref[...], dskip_ref[...], normg_ref[...]).astype(y_ref.dtype)

    @pl.when(c == nc - 1)
    def _():
        st_ref[0] = new_state.T


def _ssd_prompt(xs, z, bc, dt, wx, wbc, bx, bbc, dtb, alog, dskip, normg, batch, seqlen, groups, head_dim, n):
    m, d_ssm = xs.shape
    gw = d_ssm // groups
    hg = gw // head_dim
    cw = wx.shape[0]
    q = min(128, seqlen)
    nc = seqlen // q
    assert seqlen % q == 0 and q % LANES == 0 and n == LANES and hg <= LANES
    assert gw % MXU_DIM == 0 and MXU_DIM % head_dim == 0 and cw - 1 <= SUBLANES
    row = lambda b, g, c: (b * nc + c, g)
    gcol = lambda b, g, c: (0, g)
    in_specs = [
        pl.BlockSpec((q, gw), row),
        pl.BlockSpec((q, gw), row),
        pl.BlockSpec((q, n), row),
        pl.BlockSpec((q, n), lambda b, g, c: (b * nc + c, groups + g)),
        pl.BlockSpec((q, LANES), row),
        pl.BlockSpec((cw, gw), gcol),
        pl.BlockSpec((cw, n), gcol),
        pl.BlockSpec((cw, n), lambda b, g, c: (0, groups + g)),
        pl.BlockSpec((1, gw), gcol),
        pl.BlockSpec((1, n), gcol),
        pl.BlockSpec((1, n), lambda b, g, c: (0, groups + g)),
        pl.BlockSpec((1, LANES), gcol),
        pl.BlockSpec((1, LANES), gcol),
        pl.BlockSpec((1, gw), gcol),
        pl.BlockSpec((1, gw), gcol),
        pl.BlockSpec((LANES, gw), lambda b, g, c: (0, 0)),
    ]
    return pl.pallas_call(
        functools.partial(_ssd_prompt_body, q=q, nc=nc, head_dim=head_dim, cw=cw),
        grid=(batch, groups, nc),
        in_specs=in_specs,
        out_specs=[pl.BlockSpec((q, gw), row),
                   pl.BlockSpec((1, gw, n), lambda b, g, c: (b, g, 0))],
        out_shape=[jax.ShapeDtypeStruct((m, d_ssm), BF16),
                   jax.ShapeDtypeStruct((batch, d_ssm, n), F32)],
        scratch_shapes=[pltpu.VMEM((n, gw), F32),
                        pltpu.VMEM((SUBLANES + q, gw), F32),
                        pltpu.VMEM((SUBLANES + q, n), F32),
                        pltpu.VMEM((SUBLANES + q, n), F32)],
        compiler_params=_params(("arbitrary", "arbitrary", "arbitrary"), 32 * 1024 * 1024),
        name="ssd_prompt",
    )(xs, z, bc, bc, dt, wx, wbc, wbc, bx, bbc, bbc, dtb, alog, dskip, normg,
      _head_expand_matrix(hg, head_dim))


def _ssm_sample_pre_body(xs_ref, b_ref, c_ref, dt_ref, hx_ref, hb_ref, hc_ref, wx_ref, wb_ref, wc_ref,
                         bx_ref, bb_ref, bc_ref, dtb_ref, alog_ref, expand_ref,
                         xs_o, xdt_o, b_o, c_o, decay_o, *, cw):
    def conv(new_ref, hist_ref, w_ref, bias_ref):
        acc = bias_ref[...] + w_ref[cw - 1:cw, :] * new_ref[...]
        for k in range(cw - 1):
            acc = acc + w_ref[k:k + 1, :] * hist_ref[k]
        return _silu(acc)

    xs = conv(xs_ref, hx_ref, wx_ref, bx_ref)
    b_o[...] = conv(b_ref, hb_ref, wb_ref, bb_ref)
    c_o[...] = conv(c_ref, hc_ref, wc_ref, bc_ref)
    dt = jax.nn.softplus(dt_ref[...] + dtb_ref[...])
    decay_o[...] = jnp.exp(dt * (-jnp.exp(alog_ref[...])))
    dt_x = jnp.dot(dt, expand_ref[...], precision=HIGHEST, preferred_element_type=F32)
    xs_o[...] = xs
    xdt_o[...] = (xs * dt_x).astype(xdt_o.dtype)


def _ssm_sample_pre(xs, bc, dt, hist, wx, wbc, bx, bbc, dtb, alog, groups, head_dim, n):
    nb, d_ssm = xs.shape
    gw = d_ssm // groups
    hg = gw // head_dim
    cw = wx.shape[0]
    xoff = d_ssm // n
    gcol = lambda g: (0, g)
    in_specs = [
        pl.BlockSpec((nb, gw), gcol),
        pl.BlockSpec((nb, n), gcol),
        pl.BlockSpec((nb, n), lambda g: (0, groups + g)),
        pl.BlockSpec((nb, LANES), gcol),
        pl.BlockSpec((cw - 1, nb, gw), lambda g: (0, 0, g)),
        pl.BlockSpec((cw - 1, nb, n), lambda g: (0, 0, xoff + g)),
        pl.BlockSpec((cw - 1, nb, n), lambda g: (0, 0, xoff + groups + g)),
        pl.BlockSpec((cw, gw), gcol),
        pl.BlockSpec((cw, n), gcol),
        pl.BlockSpec((cw, n), lambda g: (0, groups + g)),
        pl.BlockSpec((1, gw), gcol),
        pl.BlockSpec((1, n), gcol),
        pl.BlockSpec((1, n), lambda g: (0, groups + g)),
        pl.BlockSpec((1, LANES), gcol),
        pl.BlockSpec((1, LANES), gcol),
        pl.BlockSpec((LANES, gw), lambda g: (0, 0)),
    ]
    return pl.pallas_call(
        functools.partial(_ssm_sample_pre_body, cw=cw),
        grid=(groups,),
        in_specs=in_specs,
        out_specs=[pl.BlockSpec((nb, gw), gcol),
                   pl.BlockSpec((nb, gw), gcol),
                   pl.BlockSpec((nb, n), gcol),
                   pl.BlockSpec((nb, n), gcol),
                   pl.BlockSpec((nb, LANES), gcol)],
        out_shape=[jax.ShapeDtypeStruct((nb, d_ssm), F32),
                   jax.ShapeDtypeStruct((nb, d_ssm), BF16),
                   jax.ShapeDtypeStruct((nb, groups * n), F32),
                   jax.ShapeDtypeStruct((nb, groups * n), F32),
                   jax.ShapeDtypeStruct((nb, groups * LANES), F32)],
        compiler_params=_params(("parallel",), 32 * 1024 * 1024),
        name="ssm_sample_pre",
    )(xs, bc, bc, dt, hist, hist, hist, wx, wbc, wbc, bx, bbc, bbc, dtb, alog,
      _head_expand_matrix(hg, head_dim))


def _ssm_sample_state_body(decay_ref, xdt_t_ref, bg_ref, cg_ref, xs_ref, z_ref, dskip_ref, normg_ref, st_ref,
                           y_ref, nst_ref, *, groups, hg, head_dim):
    b = pl.program_id(0)
    nb = bg_ref.shape[1]
    n = bg_ref.shape[2]
    gw = hg * head_dim
    heads = groups * hg
    is_b = lax.broadcasted_iota(jnp.int32, (nb, n), 0) == b
    for g in range(groups):
        lo = g * gw
        bsel = jnp.where(is_b, bg_ref[g], 0.0).astype(BF16)
        outer = jnp.dot(xdt_t_ref[lo:lo + gw, :], bsel, preferred_element_type=F32)
        for h in range(hg):
            r0 = lo + h * head_dim
            d = decay_ref[b * heads + g * hg + h]
            nst_ref[0, r0:r0 + head_dim, :] = (st_ref[0, r0:r0 + head_dim, :] * d
                                               + outer[h * head_dim:(h + 1) * head_dim, :])
        c_row = jnp.broadcast_to(cg_ref[g, pl.ds(b, 1), :], (SUBLANES, n)).astype(BF16)
        y = lax.dot_general(c_row, nst_ref[0, lo:lo + gw, :].astype(BF16), (((1,), (1,)), ((), ())),
                            preferred_element_type=F32)[0:1, :]
        y = _group_rmsnorm_gate(y, xs_ref[pl.ds(b, 1), lo:lo + gw], z_ref[pl.ds(b, 1), lo:lo + gw],
                                dskip_ref[:, lo:lo + gw], normg_ref[:, lo:lo + gw])
        y_ref[0, :, lo:lo + gw] = y.astype(y_ref.dtype)


def _ssm_sample_state(decay, xdt_t, bg, cg, xs, z, dskip, normg, state, groups, head_dim):
    nb, d_ssm = xs.shape
    n = state.shape[-1]
    hg = d_ssm // groups // head_dim
    full2 = lambda b: (0, 0)
    full3 = lambda b: (0, 0, 0)
    return pl.pallas_call(
        functools.partial(_ssm_sample_state_body, groups=groups, hg=hg, head_dim=head_dim),
        grid=(nb,),
        in_specs=[pl.BlockSpec(memory_space=pltpu.SMEM),
                  pl.BlockSpec((d_ssm, nb), full2),
                  pl.BlockSpec((groups, nb, n), full3),
                  pl.BlockSpec((groups, nb, n), full3),
                  pl.BlockSpec((nb, d_ssm), full2),
                  pl.BlockSpec((nb, d_ssm), full2),
                  pl.BlockSpec((1, d_ssm), full2),
                  pl.BlockSpec((1, d_ssm), full2),
                  pl.BlockSpec((1, d_ssm, n), lambda b: (b, 0, 0))],
        out_specs=[pl.BlockSpec((1, 1, d_ssm), lambda b: (b, 0, 0)),
                   pl.BlockSpec((1, d_ssm, n), lambda b: (b, 0, 0))],
        out_shape=[jax.ShapeDtypeStruct((nb, 1, d_ssm), F32),
                   jax.ShapeDtypeStruct((nb, d_ssm, n), F32)],
        compiler_params=_params(("arbitrary",), 48 * 1024 * 1024),
        name="ssm_sample_state",
    )(decay, xdt_t, bg, cg, xs, z, dskip, normg, state)


def _softmax_rows(s):
    e = jnp.exp(s - jnp.max(s, axis=-1, keepdims=True))
    return e / jnp.sum(e, axis=-1, keepdims=True)


def _attn_prompt_body(q_ref, k_ref, v_ref, o_ref, *, scale):
    s = lax.dot_general(q_ref[...], k_ref[...].astype(BF16), (((1,), (1,)), ((), ())),
                        preferred_element_type=F32) * scale
    p = _softmax_rows(s)
    o_ref[...] = jnp.dot(p.astype(BF16), v_ref[...].astype(BF16), preferred_element_type=F32).astype(o_ref.dtype)


def _attn_prompt(q, k, v, batch, seqlen, n_mem, heads):
    m, d = q.shape
    hd = d // heads
    tq = min(512, seqlen)
    nq = seqlen // tq
    return pl.pallas_call(
        functools.partial(_attn_prompt_body, scale=hd ** -0.5),
        grid=(batch, heads, nq),
        in_specs=[pl.BlockSpec((tq, hd), lambda b, h, i: (b * nq + i, h)),
                  pl.BlockSpec((n_mem, hd), lambda b, h, i: (b, h)),
                  pl.BlockSpec((n_mem, hd), lambda b, h, i: (b, h))],
        out_specs=pl.BlockSpec((tq, hd), lambda b, h, i: (b * nq + i, h)),
        out_shape=jax.ShapeDtypeStruct((m, d), BF16),
        compiler_params=_params(("parallel", "parallel", "parallel"), 32 * 1024 * 1024),
        name="attn_prompt",
    )(q, k, v)


def _attn_sample_body(q_ref, k_ref, v_ref, o_ref, *, scale, heads):
    hd = q_ref.shape[-1] // heads
    for h in range(heads):
        col = slice(h * hd, (h + 1) * hd)
        q8 = jnp.broadcast_to(q_ref[0, :, col], (SUBLANES, hd)).astype(BF16)
        s = lax.dot_general(q8, k_ref[0, :, col].astype(BF16), (((1,), (1,)), ((), ())),
                            preferred_element_type=F32) * scale
        p = _softmax_rows(s)
        o = jnp.dot(p.astype(BF16), v_ref[0, :, col].astype(BF16), preferred_element_type=F32)
        o_ref[0, :, col] = o[0:1, :].astype(o_ref.dtype)


def _attn_sample(q, k, v, heads):
    nb, n_mem, d = k.shape
    return pl.pallas_call(
        functools.partial(_attn_sample_body, scale=(d // heads) ** -0.5, heads=heads),
        grid=(nb,),
        in_specs=[pl.BlockSpec((1, 1, d), lambda b: (b, 0, 0)),
                  pl.BlockSpec((1, n_mem, d), lambda b: (b, 0, 0)),
                  pl.BlockSpec((1, n_mem, d), lambda b: (b, 0, 0))],
        out_specs=pl.BlockSpec((1, 1, d), lambda b: (b, 0, 0)),
        out_shape=jax.ShapeDtypeStruct((nb, 1, d), F32),
        compiler_params=_params(("parallel",), 4 * n_mem * d * 4 + 8 * 1024 * 1024),
        name="attn_sample",
    )(q, k, v)


def _layer_front(x, w, g_mix):
    xn = _rmsnorm(x, g_mix, BF16)
    mm = functools.partial(_matmul, xn, out_dtype=F32)
    return mm(w["in_glu"]), mm(w["in_z"]), mm(w["in_xs"]), mm(w["in_bc"]), mm(w["in_dt"])


def _layer_back(x, mix, attend, w, g_xattn, g_mlp, g_final):
    x = _matmul(mix, w["out"], out_dtype=F32, res=x)
    q = _matmul(_rmsnorm(x, g_xattn, BF16), w["q"], out_dtype=BF16)
    x = _matmul(attend(q), w["o"], out_dtype=F32, res=x)
    up = _matmul(_rmsnorm(x, g_mlp, BF16), w["up"], out_dtype=BF16, act="relu2")
    x = _matmul(up, w["down"], out_dtype=F32, res=x)
    return _rmsnorm(x, g_final, F32)


def kernel(x_prompt, x_sample, mem_prompt, state_conv, state_ssm_conv, state_ssm, cache_mem_k, cache_mem_v,
           g_mix, w_in, conv_w, conv_b, ln_g, ln_b, ssm_conv_w, ssm_conv_b, dt_bias, a_log, d_skip,
           ssm_norm_g, w_out, g_xattn, g_mem, w_q, w_k, w_v, w_o, g_mlp, w_up, w_down, g_final):
    depth = g_mix.shape[0]
    assert depth == 1
    batch, seqlen, d = x_prompt.shape
    nb = x_sample.shape[0]
    assert x_sample.shape[1] == 1
    d_conv = conv_w.shape[-1]
    c_xbc = ssm_conv_w.shape[-1]
    d_ssm = ssm_norm_g.shape[-1]
    heads = dt_bias.shape[-1]
    head_dim = d_ssm // heads
    n = state_ssm.shape[-1]
    groups = (c_xbc - d_ssm) // (2 * n)
    hg = heads // groups
    n_mem, xa_heads = cache_mem_k.shape[2], cache_mem_k.shape[3]
    cw = ssm_conv_w.shape[1]
    assert n == LANES and hg <= LANES

    w_in0 = w_in[0]
    o_z, o_x = 2 * d_conv, 2 * d_conv + d_ssm
    o_b, o_dt = o_x + d_ssm, o_x + c_xbc
    w_dt = jnp.pad(w_in0[:, o_dt:].reshape(d, groups, hg), ((0, 0), (0, 0), (0, LANES - hg)))
    w = {
        "in_glu": w_in0[:, :o_z].astype(BF16),
        "in_z": w_in0[:, o_z:o_x].astype(BF16),
        "in_xs": w_in0[:, o_x:o_b].astype(BF16),
        "in_bc": w_in0[:, o_b:o_dt].astype(BF16),
        "in_dt": w_dt.reshape(d, groups * LANES).astype(BF16),
        "out": w_out[0].astype(BF16),
        "q": w_q[0].astype(BF16),
        "o": w_o[0].astype(BF16),
        "up": w_up[0].astype(BF16),
        "down": w_down[0].astype(BF16),
    }
    wx, wbc = ssm_conv_w[0][:, :d_ssm], ssm_conv_w[0][:, d_ssm:]
    bx, bbc = ssm_conv_b[0][:d_ssm].reshape(1, d_ssm), ssm_conv_b[0][d_ssm:].reshape(1, c_xbc - d_ssm)
    dtb, alog = _pad_heads(dt_bias[0], groups), _pad_heads(a_log[0], groups)
    dskip = jnp.repeat(d_skip[0].astype(F32), head_dim).reshape(1, d_ssm)
    normg = ssm_norm_g[0].reshape(1, d_ssm)

    mp = batch * seqlen
    xp = x_prompt.reshape(mp, d)
    glu, z, xs, bc, dt = _layer_front(xp, w, g_mix[0])
    v_p, conv_hist = _conv_prompt(glu, conv_w[0], conv_b[0], ln_g[0], ln_b[0], batch, seqlen)
    y_p, st_p = _ssd_prompt(xs, z, bc, dt, wx, wbc, bx, bbc, dtb, alog, dskip, normg,
                            batch, seqlen, groups, head_dim, n)
    mem_n = _rmsnorm(mem_prompt.reshape(batch * n_mem, d), g_mem[0], BF16)
    k_p = _matmul(mem_n, w_k[0].astype(BF16), out_dtype=F32)
    v_mem_p = _matmul(mem_n, w_v[0].astype(BF16), out_dtype=F32)
    attend_p = lambda q: _attn_prompt(q, k_p, v_mem_p, batch, seqlen, n_mem, xa_heads)
    y_prompt = _layer_back(xp, jnp.concatenate([v_p, y_p], axis=1), attend_p, w, g_xattn[0], g_mlp[0], g_final)

    keep = conv_w.shape[1] - 1
    new_conv_p = conv_hist[:, CONV_HALO - keep:, :]
    tail = lambda a: a.reshape(batch, seqlen, -1)[:, seqlen - (cw - 1):, :]
    new_ssm_conv_p = jnp.concatenate([tail(xs), tail(bc)], axis=-1)

    xsm = x_sample.reshape(nb, d)
    glu_s, z_s, xs_s, bc_s, dt_s = _layer_front(xsm, w, g_mix[0])
    v_s, new_conv_s = _conv_sample(glu_s, state_conv[0], conv_w[0], conv_b[0], ln_g[0], ln_b[0])
    hist = jnp.swapaxes(state_ssm_conv[0], 0, 1)
    xs_c, xdt, b_c, c_c, decay = _ssm_sample_pre(xs_s, bc_s, dt_s, hist, wx, wbc, bx, bbc, dtb, alog,
                                                  groups, head_dim, n)
    decay_flat = decay.reshape(nb, groups, LANES)[:, :, :hg].reshape(nb * heads)
    to_groups = lambda a: jnp.swapaxes(a.reshape(nb, groups, n), 0, 1)
    y_s, st_s = _ssm_sample_state(decay_flat, xdt.T, to_groups(b_c), to_groups(c_c), xs_c, z_s, dskip, normg,
                                  state_ssm[0].reshape(nb, d_ssm, n), groups, head_dim)
    k_s = cache_mem_k[0].reshape(nb, n_mem, d)
    v_mem_s = cache_mem_v[0].reshape(nb, n_mem, d)
    attend_s = lambda q: _attn_sample(q.astype(F32).reshape(nb, 1, d), k_s, v_mem_s,
                                      xa_heads).reshape(nb, d).astype(BF16)
    y_sample = _layer_back(xsm, jnp.concatenate([v_s, y_s.reshape(nb, d_ssm).astype(BF16)], axis=1), attend_s,
                           w, g_xattn[0], g_mlp[0], g_final)
    new_ssm_conv_s = jnp.concatenate([state_ssm_conv[0][:, 1:, :],
                                      jnp.concatenate([xs_s, bc_s], axis=-1)[:, None, :]], axis=1)

    kv_shape = (1, batch, n_mem, xa_heads, d // xa_heads)
    return (y_prompt.reshape(batch, seqlen, d), y_sample.reshape(nb, 1, d),
            new_conv_p[None], new_ssm_conv_p[None], st_p.reshape(1, batch, heads, head_dim, n),
            k_p.reshape(kv_shape), v_mem_p.reshape(kv_shape),
            new_conv_s[None], new_ssm_conv_s[None], st_s.reshape(1, nb, heads, head_dim, n))
```

```python
import functools

import jax
import jax.numpy as jnp
from jax import lax
from jax.experimental import pallas as pl
from jax.experimental.pallas import tpu as pltpu

F32 = jnp.float32
BF16 = jnp.bfloat16
EPS = 1e-5
LANES = 128
SUBLANES = 8
MXU_DIM = 256
VMEM_CAP = 56 * 1024 * 1024


def _params(sem, vmem_bytes):
    return pltpu.CompilerParams(dimension_semantics=sem,
                                vmem_limit_bytes=int(min(VMEM_CAP, max(vmem_bytes, 16 * 1024 * 1024))))


def _sigmoid(x):
    return 0.5 * jnp.tanh(0.5 * x) + 0.5


def _silu(x):
    return x * _sigmoid(x)


def _split3(x):
    hi = x.astype(BF16)
    r1 = x - hi.astype(F32)
    mid = r1.astype(BF16)
    return hi, mid, (r1 - mid.astype(F32)).astype(BF16)


def _tile(dim, cap, unit):
    if dim <= cap:
        return dim
    best = max(t for t in range(unit, cap + 1, unit) if dim % t == 0)
    return best


def _rmsnorm_body(x_ref, g_ref, o_ref):
    x = x_ref[...]
    ms = jnp.mean(x * x, axis=-1, keepdims=True)
    o_ref[...] = (x * lax.rsqrt(ms + EPS) * g_ref[...]).astype(o_ref.dtype)


def _rmsnorm(x, g, out_dtype, bm=256):
    m, d = x.shape
    bm = min(bm, m)
    return pl.pallas_call(
        _rmsnorm_body,
        grid=(m // bm,),
        in_specs=[pl.BlockSpec((bm, d), lambda i: (i, 0)),
                  pl.BlockSpec((1, d), lambda i: (0, 0))],
        out_specs=pl.BlockSpec((bm, d), lambda i: (i, 0)),
        out_shape=jax.ShapeDtypeStruct((m, d), out_dtype),
        compiler_params=_params(("parallel",), 6 * bm * d * 4),
        name="rmsnorm",
    )(x, g.reshape(1, d).astype(F32))


def _matmul_body(*refs, part_blocks, act, has_res):
    nparts = len(part_blocks)
    nk = sum(part_blocks)
    a_refs, w_ref = refs[:nparts], refs[nparts]
    res_ref = refs[nparts + 1] if has_res else None
    o_ref = refs[nparts + 1 + has_res]
    acc_ref = refs[-1] if nk > 1 else None

    def dot(p):
        return jnp.dot(a_refs[p][...], w_ref[...], preferred_element_type=F32)

    def finish(acc):
        if act == "relu2":
            acc = jnp.square(jnp.maximum(acc, 0.0))
        if has_res:
            acc = res_ref[...] + acc
        o_ref[...] = acc.astype(o_ref.dtype)

    if nk == 1:
        finish(dot(0))
        return
    k = pl.program_id(2)
    lo = 0
    for p, blocks in enumerate(part_blocks):
        hi = lo + blocks
        first, last = max(lo, 1), min(hi, nk - 1)
        if lo == 0:
            @pl.when(k == 0)
            def _(p=p):
                acc_ref[...] = dot(p)
        if last > first:
            @pl.when((k >= first) & (k < last))
            def _(p=p):
                acc_ref[...] += dot(p)
        if hi == nk:
            @pl.when(k == nk - 1)
            def _(p=p):
                finish(acc_ref[...] + dot(p))
        lo = hi


def _matmul(a_parts, w, *, out_dtype, act=None, res=None):
    if not isinstance(a_parts, (list, tuple)):
        a_parts = [a_parts]
    m = a_parts[0].shape[0]
    kdim, n = w.shape
    assert sum(a.shape[1] for a in a_parts) == kdim
    bm = _tile(m, 1024, SUBLANES)
    if len(a_parts) == 1 and kdim <= 4096:
        bk = kdim
    else:
        bk = 2048
        while any(a.shape[1] % bk for a in a_parts):
            bk //= 2
    part_blocks = tuple(a.shape[1] // bk for a in a_parts)
    nk = sum(part_blocks)
    wide = not (res is not None and nk == 1 and bm == 1024)
    bn = _tile(n, 1024 if wide else 512, LANES)
    assert m % bm == 0 and n % bn == 0 and bk % LANES == 0
    osz = jnp.dtype(out_dtype).itemsize
    vmem = 2 * (len(a_parts) * bm * bk * 2 + bk * bn * 2 + bm * bn * osz) + 2 * bm * bn * 4
    in_specs, lo = [], 0
    for blocks in part_blocks:
        in_specs.append(pl.BlockSpec(
            (bm, bk), lambda i, j, k, lo=lo, blocks=blocks: (i, jnp.clip(k - lo, 0, blocks - 1))))
        lo += blocks
    in_specs.append(pl.BlockSpec((bk, bn), lambda i, j, k: (k, j)))
    args = [*a_parts, w]
    if res is not None:
        in_specs.append(pl.BlockSpec((bm, bn), lambda i, j, k: (i, j)))
        args.append(res)
        vmem += 2 * bm * bn * 4
    scratch = [pltpu.VMEM((bm, bn), F32)] if nk > 1 else []
    return pl.pallas_call(
        functools.partial(_matmul_body, part_blocks=part_blocks, act=act, has_res=res is not None),
        grid=(m // bm, n // bn, nk),
        in_specs=in_specs,
        out_specs=pl.BlockSpec((bm, bn), lambda i, j, k: (i, j)),
        out_shape=jax.ShapeDtypeStruct((m, n), out_dtype),
        scratch_shapes=scratch,
        compiler_params=_params(("parallel", "parallel", "arbitrary"), vmem + 4 * 1024 * 1024),
        name="matmul",
    )(*args)


CONV_HALO = 32
CONV_ROWS = 32
CONV_STRIP = 512


def _conv_prompt_body(a_ref, g_ref, w_ref, b_ref, lng_ref, lnb_ref, v_ref, hist_ref, ubuf, sbuf, cbuf,
                      *, tile, width, nt):
    t = pl.program_id(1)
    ch = ubuf.shape[1]

    @pl.when(t == 0)
    def _():
        ubuf[0:CONV_HALO, :] = jnp.zeros((CONV_HALO, ch), F32)

    ubuf[CONV_HALO:CONV_HALO + tile, :] = a_ref[...] * _sigmoid(g_ref[...])
    off0 = CONV_HALO - (width - 1)
    srows = CONV_HALO + tile - SUBLANES

    def strip(c, carry):
        col = pl.ds(pl.multiple_of(c * CONV_STRIP, CONV_STRIP), CONV_STRIP)
        for r in range(1, SUBLANES):
            sbuf[r, 0:srows, :] = ubuf[pl.ds(r, srows), col]
        for r0 in range(0, tile, CONV_ROWS):
            acc = jnp.broadcast_to(b_ref[:, col], (CONV_ROWS, CONV_STRIP))
            for k in range(width):
                whole, r = divmod(off0 + k, SUBLANES)
                if r == 0:
                    rows = ubuf[pl.ds(r0 + off0 + k, CONV_ROWS), col]
                else:
                    rows = sbuf[r, pl.ds(r0 + whole * SUBLANES, CONV_ROWS), :]
                acc = acc + w_ref[k:k + 1, col] * rows
            cbuf[r0:r0 + CONV_ROWS, col] = acc
        return carry

    lax.fori_loop(0, ch // CONV_STRIP, strip, 0)

    y = cbuf[...]
    yc = y - jnp.mean(y, axis=-1, keepdims=True)
    yn = yc * lax.rsqrt(jnp.mean(yc * yc, axis=-1, keepdims=True) + EPS)
    v_ref[...] = _silu(yn * lng_ref[...] + lnb_ref[...]).astype(v_ref.dtype)

    ubuf[0:CONV_HALO, :] = ubuf[tile:tile + CONV_HALO, :]

    @pl.when(t == nt - 1)
    def _():
        hist_ref[0] = ubuf[0:CONV_HALO, :]


def _conv_prompt(glu, conv_w, conv_b, ln_g, ln_b, batch, seqlen):
    width, ch = conv_w.shape
    tile = min(128, seqlen)
    nt = seqlen // tile
    assert width - 1 <= CONV_HALO <= tile and seqlen % tile == 0
    assert ch % CONV_STRIP == 0 and tile % CONV_ROWS == 0
    row = lambda b, t: (b * nt + t, 0)
    vec = pl.BlockSpec((1, ch), lambda b, t: (0, 0))
    return pl.pallas_call(
        functools.partial(_conv_prompt_body, tile=tile, width=width, nt=nt),
        grid=(batch, nt),
        in_specs=[pl.BlockSpec((tile, ch), row),
                  pl.BlockSpec((tile, ch), lambda b, t: (b * nt + t, 1)),
                  pl.BlockSpec((width, ch), lambda b, t: (0, 0)),
                  vec, vec, vec],
        out_specs=[pl.BlockSpec((tile, ch), row),
                   pl.BlockSpec((1, CONV_HALO, ch), lambda b, t: (b, 0, 0))],
        out_shape=[jax.ShapeDtypeStruct((batch * seqlen, ch), BF16),
                   jax.ShapeDtypeStruct((batch, CONV_HALO, ch), F32)],
        scratch_shapes=[pltpu.VMEM((CONV_HALO + tile, ch), F32),
                        pltpu.VMEM((SUBLANES, CONV_HALO + tile, CONV_STRIP), F32),
                        pltpu.VMEM((tile, ch), F32)],
        compiler_params=_params(("arbitrary", "arbitrary"), 32 * 1024 * 1024),
        name="conv_prompt",
    )(glu, glu, conv_w, conv_b.reshape(1, ch), ln_g.reshape(1, ch), ln_b.reshape(1, ch))


def _conv_sample_body(a_ref, g_ref, st_ref, w_ref, b_ref, lng_ref, lnb_ref, v_ref, nst_ref, *, width):
    u = a_ref[...] * _sigmoid(g_ref[...])
    hist = st_ref[0]
    w = w_ref[...]
    y = jnp.sum(hist * w[None, 0:width - 1, :], axis=1) + u * w[width - 1:width, :] + b_ref[...]
    yc = y - jnp.mean(y, axis=-1, keepdims=True)
    yn = yc * lax.rsqrt(jnp.mean(yc * yc, axis=-1, keepdims=True) + EPS)
    v_ref[...] = _silu(yn * lng_ref[...] + lnb_ref[...]).astype(v_ref.dtype)
    nst_ref[0, :, 0:width - 2, :] = hist[:, 1:width - 1, :]
    nst_ref[0, :, width - 2:width - 1, :] = u[:, None, :]


def _conv_sample(glu, state, conv_w, conv_b, ln_g, ln_b):
    width, ch = conv_w.shape
    nb = glu.shape[0]
    bs = SUBLANES
    assert nb % bs == 0
    vec = pl.BlockSpec((1, ch), lambda i: (0, 0))
    return pl.pallas_call(
        functools.partial(_conv_sample_body, width=width),
        grid=(nb // bs,),
        in_specs=[pl.BlockSpec((bs, ch), lambda i: (i, 0)),
                  pl.BlockSpec((bs, ch), lambda i: (i, 1)),
                  pl.BlockSpec((1, bs, width - 1, ch), lambda i: (0, i, 0, 0)),
                  pl.BlockSpec((width, ch), lambda i: (0, 0)),
                  vec, vec, vec],
        out_specs=[pl.BlockSpec((bs, ch), lambda i: (i, 0)),
                   pl.BlockSpec((1, bs, width - 1, ch), lambda i: (0, i, 0, 0))],
        out_shape=[jax.ShapeDtypeStruct((nb, ch), BF16),
                   jax.ShapeDtypeStruct(state.shape, F32)],
        compiler_params=_params(("parallel",), 32 * 1024 * 1024),
        name="conv_sample",
    )(glu, glu, state, conv_w, conv_b.reshape(1, ch), ln_g.reshape(1, ch), ln_b.reshape(1, ch))


def _expand_heads(v, heads, head_dim):
    rows = v.shape[0]
    per_tile = LANES // head_dim
    lane = lax.broadcasted_iota(jnp.int32, (rows, LANES), 1)
    tiles = []
    for j in range(heads // per_tile):
        h0 = j * per_tile
        t = jnp.broadcast_to(v[:, h0:h0 + 1], (rows, LANES))
        for i in range(1, per_tile):
            t = jnp.where(lane >= i * head_dim, jnp.broadcast_to(v[:, h0 + i:h0 + i + 1], (rows, LANES)), t)
        tiles.append(t)
    return jnp.concatenate(tiles, axis=1)


def _pad_heads(v, groups):
    hg = v.shape[0] // groups
    return jnp.pad(v.reshape(groups, hg).astype(F32), ((0, 0), (0, LANES - hg))).reshape(1, groups * LANES)


def _group_rmsnorm_gate(y, xs, z, dskip, normg):
    y = (y + dskip * xs) * _silu(z)
    return y * lax.rsqrt(jnp.mean(y * y, axis=-1, keepdims=True) + EPS) * normg


def _ssd_prompt_body(xs_ref, z_ref, b_ref, c_ref, dt_ref, wx_ref, wb_ref, wc_ref, bx_ref, bb_ref, bc_ref,
                     dtb_ref, alog_ref, dskip_ref, normg_ref,
                     y_ref, st_ref, state, xbuf, bbuf, cbuf, *, q, nc, head_dim, cw):
    c = pl.program_id(2)
    gw = xbuf.shape[1]
    n = bbuf.shape[1]
    hg = gw // head_dim
    halo = SUBLANES

    @pl.when(c == 0)
    def _():
        state[...] = jnp.zeros(state.shape, F32)
        xbuf[0:halo, :] = jnp.zeros((halo, gw), F32)
        bbuf[0:halo, :] = jnp.zeros((halo, n), F32)
        cbuf[0:halo, :] = jnp.zeros((halo, n), F32)

    def causal_conv(raw_ref, buf, w_ref, bias_ref):
        buf[halo:halo + q, :] = raw_ref[...]
        rows = buf[...]
        acc = bias_ref[...] + w_ref[cw - 1:cw, :] * rows[halo:]
        for j in range(1, cw):
            acc = acc + w_ref[cw - 1 - j:cw - j, :] * pltpu.roll(rows, j, axis=0)[halo:]
        buf[0:halo, :] = rows[q:q + halo]
        return _silu(acc)

    xs = causal_conv(xs_ref, xbuf, wx_ref, bx_ref)
    bm = causal_conv(b_ref, bbuf, wb_ref, bb_ref)
    cm = causal_conv(c_ref, cbuf, wc_ref, bc_ref)

    dt = jax.nn.softplus(dt_ref[...] + dtb_ref[...])
    dta = dt * (-jnp.exp(alog_ref[...]))
    qi = lax.broadcasted_iota(jnp.int32, (q, q), 0)
    si = lax.broadcasted_iota(jnp.int32, (q, q), 1)
    causal = qi >= si
    tri = causal.astype(BF16)
    hi, mid, lo = _split3(dta)
    cs = (jnp.dot(tri, hi, preferred_element_type=F32) + jnp.dot(tri, mid, preferred_element_type=F32)
          + jnp.dot(tri, lo, preferred_element_type=F32))
    cs_t = cs.T
    dt_t = dt.T
    exp_cs_x = _expand_heads(jnp.exp(cs), hg, head_dim)
    xdt_end = xs * _expand_heads(dt * jnp.exp(cs[q - 1:q, :] - cs), hg, head_dim)

    xs16 = xs.astype(BF16)
    cm16 = cm.astype(BF16)
    cb = lax.dot_general(cm16, bm.astype(BF16), (((1,), (1,)), ((), ())), preferred_element_type=F32)

    hb = MXU_DIM // head_dim
    slab = hb * head_dim
    ri = lax.broadcasted_iota(jnp.int32, (hb * q, slab), 0) // q
    ci = lax.broadcasted_iota(jnp.int32, (hb * q, slab), 1) // head_dim
    blockdiag = ri == ci
    y_parts = []
    for s in range(hg // hb):
        lhs = []
        for j in range(hb):
            h = s * hb + j
            seg = cs[:, h:h + 1] - cs_t[h:h + 1, :]
            lmat = jnp.where(causal, jnp.exp(jnp.where(causal, seg, 0.0)), 0.0)
            lhs.append((cb * lmat * dt_t[h:h + 1, :]).astype(BF16))
        xslab = xs16[:, s * slab:(s + 1) * slab]
        rhs = jnp.where(blockdiag, jnp.concatenate([xslab] * hb, axis=0), jnp.zeros((), BF16))
        y_parts.append(jnp.dot(jnp.concatenate(lhs, axis=1), rhs, preferred_element_type=F32))
    y = jnp.concatenate(y_parts, axis=1)

    st = state[...]
    y = y + jnp.dot(cm16, st.astype(BF16), preferred_element_type=F32) * exp_cs_x
    upd = jnp.dot(bm.T.astype(BF16), xdt_end.astype(BF16), preferred_element_type=F32)
    new_state = st * exp_cs_x[q - 1:q, :] + upd
    state[...] = new_state

    y_ref[...] = _group_rmsnorm_gate(y, xs, z_ref[...], dskip_ref[...], normg_ref[...]).astype(y_ref.dtype)

    @pl.when(c == nc - 1)
    def _():
        st_ref[0] = new_state.T


def _ssd_prompt(xs, z, bc, dt, wx, wbc, bx, bbc, dtb, alog, dskip, normg, batch, seqlen, groups, head_dim, n):
    m, d_ssm = xs.shape
    gw = d_ssm // groups
    hg = gw // head_dim
    cw = wx.shape[0]
    q = min(128, seqlen)
    nc = seqlen // q
    assert seqlen % q == 0 and q % LANES == 0 and n == LANES and hg <= LANES
    assert gw % MXU_DIM == 0 and MXU_DIM % head_dim == 0 and cw - 1 <= SUBLANES
    row = lambda b, g, c: (b * nc + c, g)
    gcol = lambda b, g, c: (0, g)
    in_specs = [
        pl.BlockSpec((q, gw), row),
        pl.BlockSpec((q, gw), row),
        pl.BlockSpec((q, n), row),
        pl.BlockSpec((q, n), lambda b, g, c: (b * nc + c, groups + g)),
        pl.BlockSpec((q, LANES), row),
        pl.BlockSpec((cw, gw), gcol),
        pl.BlockSpec((cw, n), gcol),
        pl.BlockSpec((cw, n), lambda b, g, c: (0, groups + g)),
        pl.BlockSpec((1, gw), gcol),
        pl.BlockSpec((1, n), gcol),
        pl.BlockSpec((1, n), lambda b, g, c: (0, groups + g)),
        pl.BlockSpec((1, LANES), gcol),
        pl.BlockSpec((1, LANES), gcol),
        pl.BlockSpec((1, gw), gcol),
        pl.BlockSpec((1, gw), gcol),
    ]
    return pl.pallas_call(
        functools.partial(_ssd_prompt_body, q=q, nc=nc, head_dim=head_dim, cw=cw),
        grid=(batch, groups, nc),
        in_specs=in_specs,
        out_specs=[pl.BlockSpec((q, gw), row),
                   pl.BlockSpec((1, gw, n), lambda b, g, c: (b, g, 0))],
        out_shape=[jax.ShapeDtypeStruct((m, d_ssm), BF16),
                   jax.ShapeDtypeStruct((batch, d_ssm, n), F32)],
        scratch_shapes=[pltpu.VMEM((n, gw), F32),
                        pltpu.VMEM((SUBLANES + q, gw), F32),
                        pltpu.VMEM((SUBLANES + q, n), F32),
                        pltpu.VMEM((SUBLANES + q, n), F32)],
        compiler_params=_params(("arbitrary", "arbitrary", "arbitrary"), 32 * 1024 * 1024),
        name="ssd_prompt",
    )(xs, z, bc, bc, dt, wx, wbc, wbc, bx, bbc, bbc, dtb, alog, dskip, normg)


def _ssm_sample_pre_body(xs_ref, b_ref, c_ref, dt_ref, hx_ref, hb_ref, hc_ref, wx_ref, wb_ref, wc_ref,
                         bx_ref, bb_ref, bc_ref, dtb_ref, alog_ref,
                         xs_o, xdt_o, b_o, c_o, decay_o, *, cw, head_dim):
    def conv(new_ref, hist_ref, w_ref, bias_ref):
        acc = bias_ref[...] + w_ref[cw - 1:cw, :] * new_ref[...]
        for k in range(cw - 1):
            acc = acc + w_ref[k:k + 1, :] * hist_ref[k]
        return _silu(acc)

    xs = conv(xs_ref, hx_ref, wx_ref, bx_ref)
    b_o[...] = conv(b_ref, hb_ref, wb_ref, bb_ref)
    c_o[...] = conv(c_ref, hc_ref, wc_ref, bc_ref)
    dt = jax.nn.softplus(dt_ref[...] + dtb_ref[...])
    decay_o[...] = jnp.exp(dt * (-jnp.exp(alog_ref[...])))
    xs_o[...] = xs
    xdt_o[...] = (xs * _expand_heads(dt, xs.shape[1] // head_dim, head_dim)).astype(xdt_o.dtype)


def _ssm_sample_pre(xs, bc, dt, hist, wx, wbc, bx, bbc, dtb, alog, groups, head_dim, n):
    nb, d_ssm = xs.shape
    gw = d_ssm // groups
    hg = gw // head_dim
    cw = wx.shape[0]
    xoff = d_ssm // n
    gcol = lambda g: (0, g)
    in_specs = [
        pl.BlockSpec((nb, gw), gcol),
        pl.BlockSpec((nb, n), gcol),
        pl.BlockSpec((nb, n), lambda g: (0, groups + g)),
        pl.BlockSpec((nb, LANES), gcol),
        pl.BlockSpec((cw - 1, nb, gw), lambda g: (0, 0, g)),
        pl.BlockSpec((cw - 1, nb, n), lambda g: (0, 0, xoff + g)),
        pl.BlockSpec((cw - 1, nb, n), lambda g: (0, 0, xoff + groups + g)),
        pl.BlockSpec((cw, gw), gcol),
        pl.BlockSpec((cw, n), gcol),
        pl.BlockSpec((cw, n), lambda g: (0, groups + g)),
        pl.BlockSpec((1, gw), gcol),
        pl.BlockSpec((1, n), gcol),
        pl.BlockSpec((1, n), lambda g: (0, groups + g)),
        pl.BlockSpec((1, LANES), gcol),
        pl.BlockSpec((1, LANES), gcol),
    ]
    return pl.pallas_call(
        functools.partial(_ssm_sample_pre_body, cw=cw, head_dim=head_dim),
        grid=(groups,),
        in_specs=in_specs,
        out_specs=[pl.BlockSpec((nb, gw), gcol),
                   pl.BlockSpec((nb, gw), gcol),
                   pl.BlockSpec((nb, n), gcol),
                   pl.BlockSpec((nb, n), gcol),
                   pl.BlockSpec((nb, LANES), gcol)],
        out_shape=[jax.ShapeDtypeStruct((nb, d_ssm), F32),
                   jax.ShapeDtypeStruct((nb, d_ssm), BF16),
                   jax.ShapeDtypeStruct((nb, groups * n), F32),
                   jax.ShapeDtypeStruct((nb, groups * n), F32),
                   jax.ShapeDtypeStruct((nb, groups * LANES), F32)],
        compiler_params=_params(("parallel",), 32 * 1024 * 1024),
        name="ssm_sample_pre",
    )(xs, bc, bc, dt, hist, hist, hist, wx, wbc, wbc, bx, bbc, bbc, dtb, alog)


def _ssm_sample_state_body(decay_ref, xdt_t_ref, bg_ref, cg_ref, xs_ref, z_ref, dskip_ref, normg_ref, st_ref,
                           y_ref, nst_ref, *, groups, hg, head_dim):
    b = pl.program_id(0)
    nb = bg_ref.shape[1]
    n = bg_ref.shape[2]
    gw = hg * head_dim
    heads = groups * hg
    is_b = lax.broadcasted_iota(jnp.int32, (nb, n), 0) == b
    for g in range(groups):
        lo = g * gw
        bsel = jnp.where(is_b, bg_ref[g], 0.0).astype(BF16)
        outer = jnp.dot(xdt_t_ref[lo:lo + gw, :], bsel, preferred_element_type=F32)
        for h in range(hg):
            r0 = lo + h * head_dim
            d = decay_ref[b * heads + g * hg + h]
            nst_ref[0, r0:r0 + head_dim, :] = (st_ref[0, r0:r0 + head_dim, :] * d
                                               + outer[h * head_dim:(h + 1) * head_dim, :])
        c_row = jnp.broadcast_to(cg_ref[g, pl.ds(b, 1), :], (SUBLANES, n)).astype(BF16)
        y = lax.dot_general(c_row, nst_ref[0, lo:lo + gw, :].astype(BF16), (((1,), (1,)), ((), ())),
                            preferred_element_type=F32)[0:1, :]
        y = _group_rmsnorm_gate(y, xs_ref[pl.ds(b, 1), lo:lo + gw], z_ref[pl.ds(b, 1), lo:lo + gw],
                                dskip_ref[:, lo:lo + gw], normg_ref[:, lo:lo + gw])
        y_ref[0, :, lo:lo + gw] = y.astype(y_ref.dtype)


def _ssm_sample_state(decay, xdt_t, bg, cg, xs, z, dskip, normg, state, groups, head_dim):
    nb, d_ssm = xs.shape
    n = state.shape[-1]
    hg = d_ssm // groups // head_dim
    full2 = lambda b: (0, 0)
    full3 = lambda b: (0, 0, 0)
    return pl.pallas_call(
        functools.partial(_ssm_sample_state_body, groups=groups, hg=hg, head_dim=head_dim),
        grid=(nb,),
        in_specs=[pl.BlockSpec(memory_space=pltpu.SMEM),
                  pl.BlockSpec((d_ssm, nb), full2),
                  pl.BlockSpec((groups, nb, n), full3),
                  pl.BlockSpec((groups, nb, n), full3),
                  pl.BlockSpec((nb, d_ssm), full2),
                  pl.BlockSpec((nb, d_ssm), full2),
                  pl.BlockSpec((1, d_ssm), full2),
                  pl.BlockSpec((1, d_ssm), full2),
                  pl.BlockSpec((1, d_ssm, n), lambda b: (b, 0, 0))],
        out_specs=[pl.BlockSpec((1, 1, d_ssm), lambda b: (b, 0, 0)),
                   pl.BlockSpec((1, d_ssm, n), lambda b: (b, 0, 0))],
        out_shape=[jax.ShapeDtypeStruct((nb, 1, d_ssm), F32),
                   jax.ShapeDtypeStruct((nb, d_ssm, n), F32)],
        compiler_params=_params(("arbitrary",), 48 * 1024 * 1024),
        name="ssm_sample_state",
    )(decay, xdt_t, bg, cg, xs, z, dskip, normg, state)


def _softmax_rows(s):
    e = jnp.exp(s - jnp.max(s, axis=-1, keepdims=True))
    return e / jnp.sum(e, axis=-1, keepdims=True)


def _attn_prompt_body(q_ref, k_ref, v_ref, o_ref, *, scale):
    s = lax.dot_general(q_ref[...], k_ref[...].astype(BF16), (((1,), (1,)), ((), ())),
                        preferred_element_type=F32) * scale
    p = _softmax_rows(s)
    o_ref[...] = jnp.dot(p.astype(BF16), v_ref[...].astype(BF16), preferred_element_type=F32).astype(o_ref.dtype)


def _attn_prompt(q, k, v, batch, seqlen, n_mem, heads):
    m, d = q.shape
    hd = d // heads
    tq = min(512, seqlen)
    nq = seqlen // tq
    return pl.pallas_call(
        functools.partial(_attn_prompt_body, scale=hd ** -0.5),
        grid=(batch, heads, nq),
        in_specs=[pl.BlockSpec((tq, hd), lambda b, h, i: (b * nq + i, h)),
                  pl.BlockSpec((n_mem, hd), lambda b, h, i: (b, h)),
                  pl.BlockSpec((n_mem, hd), lambda b, h, i: (b, h))],
        out_specs=pl.BlockSpec((tq, hd), lambda b, h, i: (b * nq + i, h)),
        out_shape=jax.ShapeDtypeStruct((m, d), BF16),
        compiler_params=_params(("parallel", "parallel", "parallel"), 32 * 1024 * 1024),
        name="attn_prompt",
    )(q, k, v)


def _attn_sample_body(q_ref, k_ref, v_ref, o_ref, *, scale, chunk):
    q = q_ref[0]
    n_mem = k_ref.shape[2]
    s = jnp.concatenate(
        [jnp.sum(k_ref[0, 0, m0:m0 + chunk] * q[None], axis=-1, keepdims=True) for m0 in range(0, n_mem, chunk)],
        axis=0) * scale
    e = jnp.exp(s - jnp.max(s, axis=0, keepdims=True))
    p = e / jnp.sum(e, axis=0, keepdims=True)
    o = jnp.sum(p[0:chunk] * v_ref[0, 0, 0:chunk], axis=0)
    for m0 in range(chunk, n_mem, chunk):
        o = o + jnp.sum(p[m0:m0 + chunk] * v_ref[0, 0, m0:m0 + chunk], axis=0)
    o_ref[0] = o


def _attn_sample(q, k, v):
    _, nb, n_mem, heads, hd = k.shape
    chunk = min(32, n_mem)
    assert n_mem % chunk == 0
    kv_spec = pl.BlockSpec((1, 1, n_mem, heads, hd), lambda b: (0, b, 0, 0, 0))
    return pl.pallas_call(
        functools.partial(_attn_sample_body, scale=hd ** -0.5, chunk=chunk),
        grid=(nb,),
        in_specs=[pl.BlockSpec((1, heads, hd), lambda b: (b, 0, 0)), kv_spec, kv_spec],
        out_specs=pl.BlockSpec((1, heads, hd), lambda b: (b, 0, 0)),
        out_shape=jax.ShapeDtypeStruct((nb, heads, hd), F32),
        compiler_params=_params(("parallel",), 4 * n_mem * SUBLANES * hd * 4 + 8 * 1024 * 1024),
        name="attn_sample",
    )(q, k, v)


def _layer_front(x, w, g_mix):
    xn = _rmsnorm(x, g_mix, BF16)
    mm = functools.partial(_matmul, xn, out_dtype=F32)
    return mm(w["in_glu"]), mm(w["in_z"]), mm(w["in_xs"]), mm(w["in_bc"]), mm(w["in_dt"])


def _layer_back(x, mix, attend, w, g_xattn, g_mlp, g_final):
    x = _matmul(mix, w["out"], out_dtype=F32, res=x)
    q = _matmul(_rmsnorm(x, g_xattn, BF16), w["q"], out_dtype=BF16)
    x = _matmul(attend(q), w["o"], out_dtype=F32, res=x)
    up = _matmul(_rmsnorm(x, g_mlp, BF16), w["up"], out_dtype=BF16, act="relu2")
    x = _matmul(up, w["down"], out_dtype=F32, res=x)
    return _rmsnorm(x, g_final, F32)


def kernel(x_prompt, x_sample, mem_prompt, state_conv, state_ssm_conv, state_ssm, cache_mem_k, cache_mem_v,
           g_mix, w_in, conv_w, conv_b, ln_g, ln_b, ssm_conv_w, ssm_conv_b, dt_bias, a_log, d_skip,
           ssm_norm_g, w_out, g_xattn, g_mem, w_q, w_k, w_v, w_o, g_mlp, w_up, w_down, g_final):
    depth = g_mix.shape[0]
    assert depth == 1
    batch, seqlen, d = x_prompt.shape
    nb = x_sample.shape[0]
    assert x_sample.shape[1] == 1
    d_conv = conv_w.shape[-1]
    c_xbc = ssm_conv_w.shape[-1]
    d_ssm = ssm_norm_g.shape[-1]
    heads = dt_bias.shape[-1]
    head_dim = d_ssm // heads
    n = state_ssm.shape[-1]
    groups = (c_xbc - d_ssm) // (2 * n)
    hg = heads // groups
    n_mem, xa_heads = cache_mem_k.shape[2], cache_mem_k.shape[3]
    cw = ssm_conv_w.shape[1]
    assert n == LANES and hg <= LANES

    w_in0 = w_in[0]
    o_z, o_x = 2 * d_conv, 2 * d_conv + d_ssm
    o_b, o_dt = o_x + d_ssm, o_x + c_xbc
    w_dt = jnp.pad(w_in0[:, o_dt:].reshape(d, groups, hg), ((0, 0), (0, 0), (0, LANES - hg)))
    w = {
        "in_glu": w_in0[:, :o_z].astype(BF16),
        "in_z": w_in0[:, o_z:o_x].astype(BF16),
        "in_xs": w_in0[:, o_x:o_b].astype(BF16),
        "in_bc": w_in0[:, o_b:o_dt].astype(BF16),
        "in_dt": w_dt.reshape(d, groups * LANES).astype(BF16),
        "out": w_out[0].astype(BF16),
        "q": w_q[0].astype(BF16),
        "o": w_o[0].astype(BF16),
        "up": w_up[0].astype(BF16),
        "down": w_down[0].astype(BF16),
    }
    wx, wbc = ssm_conv_w[0][:, :d_ssm], ssm_conv_w[0][:, d_ssm:]
    bx, bbc = ssm_conv_b[0][:d_ssm].reshape(1, d_ssm), ssm_conv_b[0][d_ssm:].reshape(1, c_xbc - d_ssm)
    dtb, alog = _pad_heads(dt_bias[0], groups), _pad_heads(a_log[0], groups)
    dskip = jnp.repeat(d_skip[0].astype(F32), head_dim).reshape(1, d_ssm)
    normg = ssm_norm_g[0].reshape(1, d_ssm)

    mp = batch * seqlen
    xp = x_prompt.reshape(mp, d)
    glu, z, xs, bc, dt = _layer_front(xp, w, g_mix[0])
    v_p, conv_hist = _conv_prompt(glu, conv_w[0], conv_b[0], ln_g[0], ln_b[0], batch, seqlen)
    y_p, st_p = _ssd_prompt(xs, z, bc, dt, wx, wbc, bx, bbc, dtb, alog, dskip, normg,
                            batch, seqlen, groups, head_dim, n)
    mem_n = _rmsnorm(mem_prompt.reshape(batch * n_mem, d), g_mem[0], BF16)
    k_p = _matmul(mem_n, w_k[0].astype(BF16), out_dtype=F32)
    v_mem_p = _matmul(mem_n, w_v[0].astype(BF16), out_dtype=F32)
    attend_p = lambda q: _attn_prompt(q, k_p, v_mem_p, batch, seqlen, n_mem, xa_heads)
    y_prompt = _layer_back(xp, [v_p, y_p], attend_p, w, g_xattn[0], g_mlp[0], g_final)

    keep = conv_w.shape[1] - 1
    new_conv_p = conv_hist[:, CONV_HALO - keep:, :]
    tail = lambda a: a.reshape(batch, seqlen, -1)[:, seqlen - (cw - 1):, :]
    new_ssm_conv_p = jnp.concatenate([tail(xs), tail(bc)], axis=-1)

    xsm = x_sample.reshape(nb, d)
    glu_s, z_s, xs_s, bc_s, dt_s = _layer_front(xsm, w, g_mix[0])
    v_s, new_conv_s = _conv_sample(glu_s, state_conv, conv_w[0], conv_b[0], ln_g[0], ln_b[0])
    hist = jnp.swapaxes(state_ssm_conv[0], 0, 1)
    xs_c, xdt, b_c, c_c, decay = _ssm_sample_pre(xs_s, bc_s, dt_s, hist, wx, wbc, bx, bbc, dtb, alog,
                                                  groups, head_dim, n)
    decay_flat = decay.reshape(nb, groups, LANES)[:, :, :hg].reshape(nb * heads)
    to_groups = lambda a: jnp.swapaxes(a.reshape(nb, groups, n), 0, 1)
    y_s, st_s = _ssm_sample_state(decay_flat, xdt.T, to_groups(b_c), to_groups(c_c), xs_c, z_s, dskip, normg,
                                  state_ssm[0].reshape(nb, d_ssm, n), groups, head_dim)
    attend_s = lambda q: _attn_sample(q.astype(F32).reshape(nb, xa_heads, d // xa_heads), cache_mem_k,
                                      cache_mem_v).reshape(nb, d).astype(BF16)
    y_sample = _layer_back(xsm, [v_s, y_s.reshape(nb, d_ssm).astype(BF16)], attend_s,
                           w, g_xattn[0], g_mlp[0], g_final)
    new_ssm_conv_s = jnp.concatenate([state_ssm_conv[0][:, 1:, :],
                                      jnp.concatenate([xs_s, bc_s], axis=-1)[:, None, :]], axis=1)

    kv_shape = (1, batch, n_mem, xa_heads, d // xa_heads)
    return (y_prompt.reshape(batch, seqlen, d), y_sample.reshape(nb, 1, d),
            new_conv_p[None], new_ssm_conv_p[None], st_p.reshape(1, batch, heads, head_dim, n),
            k_p.reshape(kv_shape), v_mem_p.reshape(kv_shape),
            new_conv_s, new_ssm_conv_s[None], st_s.reshape(1, nb, heads, head_dim, n))
```

```python
import functools

import jax
import jax.numpy as jnp
from jax import lax
from jax.experimental import pallas as pl
from jax.experimental.pallas import tpu as pltpu

F32 = jnp.float32
BF16 = jnp.bfloat16
EPS = 1e-5
LANES = 128
SUBLANES = 8
MXU_DIM = 256
VMEM_CAP = 56 * 1024 * 1024


def _params(sem, vmem_bytes):
    return pltpu.CompilerParams(dimension_semantics=sem,
                                vmem_limit_bytes=int(min(VMEM_CAP, max(vmem_bytes, 16 * 1024 * 1024))))


def _sigmoid(x):
    return 0.5 * jnp.tanh(0.5 * x) + 0.5


def _silu(x):
    return x * _sigmoid(x)


def _split3(x):
    hi = x.astype(BF16)
    r1 = x - hi.astype(F32)
    mid = r1.astype(BF16)
    return hi, mid, (r1 - mid.astype(F32)).astype(BF16)


def _tile(dim, cap, unit):
    if dim <= cap:
        return dim
    best = max(t for t in range(unit, cap + 1, unit) if dim % t == 0)
    return best


def _rmsnorm_body(x_ref, g_ref, o_ref):
    x = x_ref[...]
    ms = jnp.mean(x * x, axis=-1, keepdims=True)
    o_ref[...] = (x * lax.rsqrt(ms + EPS) * g_ref[...]).astype(o_ref.dtype)


def _rmsnorm(x, g, out_dtype, bm=256):
    m, d = x.shape
    bm = min(bm, m)
    return pl.pallas_call(
        _rmsnorm_body,
        grid=(m // bm,),
        in_specs=[pl.BlockSpec((bm, d), lambda i: (i, 0)),
                  pl.BlockSpec((1, d), lambda i: (0, 0))],
        out_specs=pl.BlockSpec((bm, d), lambda i: (i, 0)),
        out_shape=jax.ShapeDtypeStruct((m, d), out_dtype),
        compiler_params=_params(("parallel",), 6 * bm * d * 4),
        name="rmsnorm",
    )(x, g.reshape(1, d).astype(F32))


def _epilogue(acc, res_ref, o_ref, act):
    if act == "relu2":
        acc = jnp.square(jnp.maximum(acc, 0.0))
    if res_ref is not None:
        acc = res_ref[...] + acc
    o_ref[...] = acc.astype(o_ref.dtype)


def _proj_body(*refs, act, has_s, has_res, w_rows_are_outputs):
    if w_rows_are_outputs:
        mm = lambda a, w: lax.dot_general(a, w, (((1,), (1,)), ((), ())), preferred_element_type=F32)
    else:
        mm = lambda a, w: jnp.dot(a, w, preferred_element_type=F32)
    it = iter(refs)
    ap_ref = next(it)
    as_ref = next(it) if has_s else None
    w_ref = next(it)
    rp_ref = next(it) if has_res else None
    rs_ref = next(it) if has_res and has_s else None
    op_ref = next(it)
    os_ref = next(it) if has_s else None
    w16 = next(it)

    @pl.when(pl.program_id(1) == 0)
    def _():
        w16[...] = w_ref[...].astype(BF16)
        if has_s:
            _epilogue(mm(as_ref[...], w16[...]), rs_ref, os_ref, act)

    _epilogue(mm(ap_ref[...], w16[...]), rp_ref, op_ref, act)


def _proj(a_p, a_s, w, *, out_dtype, col0=0, ncols=None, act=None, res_p=None, res_s=None, w_rows_are_outputs=False):
    mp, kdim = a_p.shape
    ncols = w.shape[1 if w_rows_are_outputs else 2] - col0 if ncols is None else ncols
    bm = _tile(mp, 1024, SUBLANES)
    bn = max(t for t in range(LANES, 512 + 1, LANES) if col0 % t == 0 and ncols % t == 0)
    c0 = col0 // bn
    has_s, has_res = a_s is not None, res_p is not None
    osz = jnp.dtype(out_dtype).itemsize
    vmem = 2 * kdim * bn * 4 + kdim * bn * 2 + 2 * bm * kdim * 2 + 2 * bm * bn * osz + 2 * bm * bn * 4
    in_specs = [pl.BlockSpec((bm, kdim), lambda j, i: (i, 0))]
    args = [a_p]
    out_specs = [pl.BlockSpec((bm, bn), lambda j, i: (i, j))]
    out_shape = [jax.ShapeDtypeStruct((mp, ncols), out_dtype)]
    if has_s:
        ms = a_s.shape[0]
        in_specs.append(pl.BlockSpec((ms, kdim), lambda j, i: (0, 0)))
        args.append(a_s)
        out_specs.append(pl.BlockSpec((ms, bn), lambda j, i: (0, j)))
        out_shape.append(jax.ShapeDtypeStruct((ms, ncols), out_dtype))
        vmem += 2 * ms * kdim * 2 + 4 * ms * bn * 4
    if w_rows_are_outputs:
        in_specs.append(pl.BlockSpec((None, bn, kdim), lambda j, i: (0, c0 + j, 0)))
    else:
        in_specs.append(pl.BlockSpec((None, kdim, bn), lambda j, i: (0, 0, c0 + j)))
    args.append(w)
    if has_res:
        in_specs.append(pl.BlockSpec((bm, bn), lambda j, i: (i, j)))
        args.append(res_p)
        vmem += 2 * bm * bn * 4
        if has_s:
            in_specs.append(pl.BlockSpec((ms, bn), lambda j, i: (0, j)))
            args.append(res_s)
    out = pl.pallas_call(
        functools.partial(_proj_body, act=act, has_s=has_s, has_res=has_res, w_rows_are_outputs=w_rows_are_outputs),
        grid=(ncols // bn, mp // bm),
        in_specs=in_specs,
        out_specs=out_specs,
        out_shape=out_shape,
        scratch_shapes=[pltpu.VMEM((bn, kdim) if w_rows_are_outputs else (kdim, bn), BF16)],
        compiler_params=_params(("parallel", "arbitrary"), vmem + 2 * 1024 * 1024),
        name="proj",
    )(*args)
    return (out[0], out[1]) if has_s else (out[0], None)


def _kgrid_body(*refs, part_blocks, has_res):
    nparts = len(part_blocks)
    nk = sum(part_blocks)
    it = iter(refs)
    ap_refs = [next(it) for _ in range(nparts)]
    as_refs = [next(it) for _ in range(nparts)]
    w_ref = next(it)
    rp_ref = next(it) if has_res else None
    rs_ref = next(it) if has_res else None
    op_ref, os_ref, accp_ref, accs_ref = next(it), next(it), next(it), next(it)
    i, k = pl.program_id(1), pl.program_id(2)

    def reduce_k(a_refs, acc_ref, res_ref, o_ref):
        lo = 0
        for p, blocks in enumerate(part_blocks):
            hi = lo + blocks
            first, last = max(lo, 1), min(hi, nk - 1)
            dot = lambda p=p: jnp.dot(a_refs[p][...], w_ref[...], preferred_element_type=F32)
            if lo == 0:
                @pl.when(k == 0)
                def _(dot=dot):
                    acc_ref[...] = dot()
            if last > first:
                @pl.when((k >= first) & (k < last))
                def _(dot=dot):
                    acc_ref[...] += dot()
            if hi == nk:
                @pl.when(k == nk - 1)
                def _(dot=dot):
                    _epilogue(acc_ref[...] + dot(), res_ref, o_ref, None)
            lo = hi

    reduce_k(ap_refs, accp_ref, rp_ref, op_ref)

    @pl.when(i == 0)
    def _():
        reduce_k(as_refs, accs_ref, rs_ref, os_ref)


def _kgrid(parts_p, parts_s, w, *, res_p=None, res_s=None):
    mp, ms = parts_p[0].shape[0], parts_s[0].shape[0]
    kdim, n = w.shape
    assert sum(a.shape[1] for a in parts_p) == kdim
    bm = _tile(mp, 1024, SUBLANES)
    bn = _tile(n, 1024, LANES)
    bk = 2048
    while any(a.shape[1] % bk for a in parts_p):
        bk //= 2
    part_blocks = tuple(a.shape[1] // bk for a in parts_p)
    nk = sum(part_blocks)
    assert nk >= 2 and bk % LANES == 0
    has_res = res_p is not None
    vmem = (2 * len(parts_p) * (bm + ms) * bk * 2 + 2 * bk * bn * 2 + (5 + 2 * has_res) * (bm + ms) * bn * 4)
    lhs_specs = lambda rows, row_of: [
        pl.BlockSpec((rows, bk), lambda j, i, k, lo=lo, blocks=blocks: (row_of(i), jnp.clip(k - lo, 0, blocks - 1)))
        for lo, blocks in zip([sum(part_blocks[:p]) for p in range(len(part_blocks))], part_blocks)]
    in_specs = lhs_specs(bm, lambda i: i) + lhs_specs(ms, lambda i: 0)
    in_specs.append(pl.BlockSpec((bk, bn), lambda j, i, k: (k, j)))
    args = [*parts_p, *parts_s, w]
    if has_res:
        in_specs += [pl.BlockSpec((bm, bn), lambda j, i, k: (i, j)), pl.BlockSpec((ms, bn), lambda j, i, k: (0, j))]
        args += [res_p, res_s]
    return pl.pallas_call(
        functools.partial(_kgrid_body, part_blocks=part_blocks, has_res=has_res),
        grid=(n // bn, mp // bm, nk),
        in_specs=in_specs,
        out_specs=[pl.BlockSpec((bm, bn), lambda j, i, k: (i, j)), pl.BlockSpec((ms, bn), lambda j, i, k: (0, j))],
        out_shape=[jax.ShapeDtypeStruct((mp, n), F32), jax.ShapeDtypeStruct((ms, n), F32)],
        scratch_shapes=[pltpu.VMEM((bm, bn), F32), pltpu.VMEM((ms, bn), F32)],
        compiler_params=_params(("parallel", "arbitrary", "arbitrary"), vmem + 2 * 1024 * 1024),
        name="kgrid",
    )(*args)


CONV_HALO = 32
CONV_ROWS = 32
CONV_STRIP = 512


def _conv_prompt_body(a_ref, g_ref, w_ref, b_ref, lng_ref, lnb_ref, v_ref, hist_ref, ubuf, sbuf, cbuf,
                      *, tile, width, nt):
    t = pl.program_id(1)
    ch = ubuf.shape[1]

    @pl.when(t == 0)
    def _():
        ubuf[0:CONV_HALO, :] = jnp.zeros((CONV_HALO, ch), F32)

    ubuf[CONV_HALO:CONV_HALO + tile, :] = a_ref[...] * _sigmoid(g_ref[...])
    off0 = CONV_HALO - (width - 1)
    srows = CONV_HALO + tile - SUBLANES

    def strip(c, carry):
        col = pl.ds(pl.multiple_of(c * CONV_STRIP, CONV_STRIP), CONV_STRIP)
        for r in range(1, SUBLANES):
            sbuf[r, 0:srows, :] = ubuf[pl.ds(r, srows), col]
        for r0 in range(0, tile, CONV_ROWS):
            acc = jnp.broadcast_to(b_ref[:, col], (CONV_ROWS, CONV_STRIP))
            for k in range(width):
                whole, r = divmod(off0 + k, SUBLANES)
                if r == 0:
                    rows = ubuf[pl.ds(r0 + off0 + k, CONV_ROWS), col]
                else:
                    rows = sbuf[r, pl.ds(r0 + whole * SUBLANES, CONV_ROWS), :]
                acc = acc + w_ref[k:k + 1, col] * rows
            cbuf[r0:r0 + CONV_ROWS, col] = acc
        return carry

    lax.fori_loop(0, ch // CONV_STRIP, strip, 0)

    y = cbuf[...]
    yc = y - jnp.mean(y, axis=-1, keepdims=True)
    yn = yc * lax.rsqrt(jnp.mean(yc * yc, axis=-1, keepdims=True) + EPS)
    v_ref[...] = _silu(yn * lng_ref[...] + lnb_ref[...]).astype(v_ref.dtype)

    ubuf[0:CONV_HALO, :] = ubuf[tile:tile + CONV_HALO, :]

    @pl.when(t == nt - 1)
    def _():
        hist_ref[0] = ubuf[0:CONV_HALO, :]


def _conv_prompt(glu, conv_w, conv_b, ln_g, ln_b, batch, seqlen):
    width, ch = conv_w.shape
    tile = min(128, seqlen)
    nt = seqlen // tile
    assert width - 1 <= CONV_HALO <= tile and seqlen % tile == 0
    assert ch % CONV_STRIP == 0 and tile % CONV_ROWS == 0
    row = lambda b, t: (b * nt + t, 0)
    vec = pl.BlockSpec((1, ch), lambda b, t: (0, 0))
    return pl.pallas_call(
        functools.partial(_conv_prompt_body, tile=tile, width=width, nt=nt),
        grid=(batch, nt),
        in_specs=[pl.BlockSpec((tile, ch), row),
                  pl.BlockSpec((tile, ch), lambda b, t: (b * nt + t, 1)),
                  pl.BlockSpec((width, ch), lambda b, t: (0, 0)),
                  vec, vec, vec],
        out_specs=[pl.BlockSpec((tile, ch), row),
                   pl.BlockSpec((1, CONV_HALO, ch), lambda b, t: (b, 0, 0))],
        out_shape=[jax.ShapeDtypeStruct((batch * seqlen, ch), BF16),
                   jax.ShapeDtypeStruct((batch, CONV_HALO, ch), F32)],
        scratch_shapes=[pltpu.VMEM((CONV_HALO + tile, ch), F32),
                        pltpu.VMEM((SUBLANES, CONV_HALO + tile, CONV_STRIP), F32),
                        pltpu.VMEM((tile, ch), F32)],
        compiler_params=_params(("arbitrary", "arbitrary"), 32 * 1024 * 1024),
        name="conv_prompt",
    )(glu, glu, conv_w, conv_b.reshape(1, ch), ln_g.reshape(1, ch), ln_b.reshape(1, ch))


def _conv_sample_body(a_ref, g_ref, st_ref, w_ref, b_ref, lng_ref, lnb_ref, v_ref, nst_ref, *, width):
    u = a_ref[...] * _sigmoid(g_ref[...])
    y = b_ref[...] + w_ref[width - 1:width, :] * u
    for k in range(width - 1):
        y = y + w_ref[k:k + 1, :] * st_ref[k]
    yc = y - jnp.mean(y, axis=-1, keepdims=True)
    yn = yc * lax.rsqrt(jnp.mean(yc * yc, axis=-1, keepdims=True) + EPS)
    v_ref[...] = _silu(yn * lng_ref[...] + lnb_ref[...]).astype(v_ref.dtype)
    for k in range(width - 2):
        nst_ref[k] = st_ref[k + 1]
    nst_ref[width - 2] = u


def _conv_sample(glu, state, conv_w, conv_b, ln_g, ln_b):
    width, ch = conv_w.shape
    nb = glu.shape[0]
    bs = _tile(nb, 2 * SUBLANES, SUBLANES)
    assert nb % bs == 0
    vec = pl.BlockSpec((1, ch), lambda i: (0, 0))
    return pl.pallas_call(
        functools.partial(_conv_sample_body, width=width),
        grid=(nb // bs,),
        in_specs=[pl.BlockSpec((bs, ch), lambda i: (i, 0)),
                  pl.BlockSpec((bs, ch), lambda i: (i, 1)),
                  pl.BlockSpec((width - 1, bs, ch), lambda i: (0, i, 0)),
                  pl.BlockSpec((width, ch), lambda i: (0, 0)),
                  vec, vec, vec],
        out_specs=[pl.BlockSpec((bs, ch), lambda i: (i, 0)),
                   pl.BlockSpec((width - 1, bs, ch), lambda i: (0, i, 0))],
        out_shape=[jax.ShapeDtypeStruct((nb, ch), BF16),
                   jax.ShapeDtypeStruct(state.shape, F32)],
        compiler_params=_params(("parallel",), 32 * 1024 * 1024),
        name="conv_sample",
    )(glu, glu, state, conv_w, conv_b.reshape(1, ch), ln_g.reshape(1, ch), ln_b.reshape(1, ch))


def _expand_heads(v, heads, head_dim):
    rows = v.shape[0]
    per_tile = LANES // head_dim
    lane = lax.broadcasted_iota(jnp.int32, (rows, LANES), 1)
    tiles = []
    for j in range(heads // per_tile):
        h0 = j * per_tile
        t = jnp.broadcast_to(v[:, h0:h0 + 1], (rows, LANES))
        for i in range(1, per_tile):
            t = jnp.where(lane >= i * head_dim, jnp.broadcast_to(v[:, h0 + i:h0 + i + 1], (rows, LANES)), t)
        tiles.append(t)
    return jnp.concatenate(tiles, axis=1)


def _pad_heads(v, groups):
    hg = v.shape[0] // groups
    return jnp.pad(v.reshape(groups, hg).astype(F32), ((0, 0), (0, LANES - hg))).reshape(1, groups * LANES)


def _group_rmsnorm_gate(y, xs, z, dskip, normg):
    y = (y + dskip * xs) * _silu(z)
    return y * lax.rsqrt(jnp.mean(y * y, axis=-1, keepdims=True) + EPS) * normg


def _ssd_prompt_body(xs_ref, z_ref, b_ref, c_ref, dt_ref, wx_ref, wb_ref, wc_ref, bx_ref, bb_ref, bc_ref,
                     dtb_ref, alog_ref, dskip_ref, normg_ref,
                     y_ref, st_ref, state, xbuf, bbuf, cbuf, *, q, nc, head_dim, cw):
    c = pl.program_id(2)
    gw = xbuf.shape[1]
    n = bbuf.shape[1]
    hg = gw // head_dim
    halo = SUBLANES

    @pl.when(c == 0)
    def _():
        state[...] = jnp.zeros(state.shape, F32)
        xbuf[0:halo, :] = jnp.zeros((halo, gw), F32)
        bbuf[0:halo, :] = jnp.zeros((halo, n), F32)
        cbuf[0:halo, :] = jnp.zeros((halo, n), F32)

    def causal_conv(raw_ref, buf, w_ref, bias_ref):
        buf[halo:halo + q, :] = raw_ref[...]
        rows = buf[...]
        acc = bias_ref[...] + w_ref[cw - 1:cw, :] * rows[halo:]
        for j in range(1, cw):
            acc = acc + w_ref[cw - 1 - j:cw - j, :] * pltpu.roll(rows, j, axis=0)[halo:]
        buf[0:halo, :] = rows[q:q + halo]
        return _silu(acc)

    xs = causal_conv(xs_ref, xbuf, wx_ref, bx_ref)
    bm = causal_conv(b_ref, bbuf, wb_ref, bb_ref)
    cm = causal_conv(c_ref, cbuf, wc_ref, bc_ref)

    dt = jax.nn.softplus(dt_ref[...] + dtb_ref[...])
    dta = dt * (-jnp.exp(alog_ref[...]))
    qi = lax.broadcasted_iota(jnp.int32, (q, q), 0)
    si = lax.broadcasted_iota(jnp.int32, (q, q), 1)
    causal = qi >= si
    tri = causal.astype(BF16)
    hi, mid, lo = _split3(dta)
    cs = (jnp.dot(tri, hi, preferred_element_type=F32) + jnp.dot(tri, mid, preferred_element_type=F32)
          + jnp.dot(tri, lo, preferred_element_type=F32))
    cs_t = cs.T
    dt_t = dt.T
    exp_cs_x = _expand_heads(jnp.exp(cs), hg, head_dim)
    xdt_end = xs * _expand_heads(dt * jnp.exp(cs[q - 1:q, :] - cs), hg, head_dim)

    xs16 = xs.astype(BF16)
    cm16 = cm.astype(BF16)
    cb = lax.dot_general(cm16, bm.astype(BF16), (((1,), (1,)), ((), ())), preferred_element_type=F32)

    hb = MXU_DIM // head_dim
    slab = hb * head_dim
    ri = lax.broadcasted_iota(jnp.int32, (hb * q, slab), 0) // q
    ci = lax.broadcasted_iota(jnp.int32, (hb * q, slab), 1) // head_dim
    blockdiag = ri == ci
    y_parts = []
    for s in range(hg // hb):
        lhs = []
        for j in range(hb):
            h = s * hb + j
            seg = cs[:, h:h + 1] - cs_t[h:h + 1, :]
            lmat = jnp.where(causal, jnp.exp(jnp.where(causal, seg, 0.0)), 0.0)
            lhs.append((cb * lmat * dt_t[h:h + 1, :]).astype(BF16))
        xslab = xs16[:, s * slab:(s + 1) * slab]
        rhs = jnp.where(blockdiag, jnp.concatenate([xslab] * hb, axis=0), jnp.zeros((), BF16))
        y_parts.append(jnp.dot(jnp.concatenate(lhs, axis=1), rhs, preferred_element_type=F32))
    y = jnp.concatenate(y_parts, axis=1)

    st = state[...]
    y = y + jnp.dot(cm16, st.astype(BF16), preferred_element_type=F32) * exp_cs_x
    upd = jnp.dot(bm.T.astype(BF16), xdt_end.astype(BF16), preferred_element_type=F32)
    new_state = st * exp_cs_x[q - 1:q, :] + upd
    state[...] = new_state

    y_ref[...] = _group_rmsnorm_gate(y, xs, z_ref[...], dskip_ref[...], normg_ref[...]).astype(y_ref.dtype)

    @pl.when(c == nc - 1)
    def _():
        st_ref[0] = new_state.T


def _ssd_prompt(xs, z, bc, dt, wx, wbc, bx, bbc, dtb, alog, dskip, normg, batch, seqlen, groups, head_dim, n):
    m, d_ssm = xs.shape
    gw = d_ssm // groups
    hg = gw // head_dim
    cw = wx.shape[0]
    q = min(128, seqlen)
    nc = seqlen // q
    assert seqlen % q == 0 and q % LANES == 0 and n == LANES and hg <= LANES
    assert gw % MXU_DIM == 0 and MXU_DIM % head_dim == 0 and cw - 1 <= SUBLANES
    row = lambda b, g, c: (b * nc + c, g)
    gcol = lambda b, g, c: (0, g)
    in_specs = [
        pl.BlockSpec((q, gw), row),
        pl.BlockSpec((q, gw), row),
        pl.BlockSpec((q, n), row),
        pl.BlockSpec((q, n), lambda b, g, c: (b * nc + c, groups + g)),
        pl.BlockSpec((q, LANES), row),
        pl.BlockSpec((cw, gw), gcol),
        pl.BlockSpec((cw, n), gcol),
        pl.BlockSpec((cw, n), lambda b, g, c: (0, groups + g)),
        pl.BlockSpec((1, gw), gcol),
        pl.BlockSpec((1, n), gcol),
        pl.BlockSpec((1, n), lambda b, g, c: (0, groups + g)),
        pl.BlockSpec((1, LANES), gcol),
        pl.BlockSpec((1, LANES), gcol),
        pl.BlockSpec((1, gw), gcol),
        pl.BlockSpec((1, gw), gcol),
    ]
    return pl.pallas_call(
        functools.partial(_ssd_prompt_body, q=q, nc=nc, head_dim=head_dim, cw=cw),
        grid=(batch, groups, nc),
        in_specs=in_specs,
        out_specs=[pl.BlockSpec((q, gw), row),
                   pl.BlockSpec((1, gw, n), lambda b, g, c: (b, g, 0))],
        out_shape=[jax.ShapeDtypeStruct((m, d_ssm), BF16),
                   jax.ShapeDtypeStruct((batch, d_ssm, n), F32)],
        scratch_shapes=[pltpu.VMEM((n, gw), F32),
                        pltpu.VMEM((SUBLANES + q, gw), F32),
                        pltpu.VMEM((SUBLANES + q, n), F32),
                        pltpu.VMEM((SUBLANES + q, n), F32)],
        compiler_params=_params(("arbitrary", "arbitrary", "arbitrary"), 32 * 1024 * 1024),
        name="ssd_prompt",
    )(xs, z, bc, bc, dt, wx, wbc, wbc, bx, bbc, bbc, dtb, alog, dskip, normg)


def _ssm_sample_pre_body(xs_ref, b_ref, c_ref, dt_ref, hx_ref, hb_ref, hc_ref, wx_ref, wb_ref, wc_ref,
                         bx_ref, bb_ref, bc_ref, dtb_ref, alog_ref,
                         xs_o, xdt_o, b_o, c_o, decay_o, *, cw, head_dim):
    def conv(new_ref, hist_ref, w_ref, bias_ref):
        acc = bias_ref[...] + w_ref[cw - 1:cw, :] * new_ref[...]
        for k in range(cw - 1):
            acc = acc + w_ref[k:k + 1, :] * hist_ref[k]
        return _silu(acc)

    xs = conv(xs_ref, hx_ref, wx_ref, bx_ref)
    b_o[...] = conv(b_ref, hb_ref, wb_ref, bb_ref)
    c_o[...] = conv(c_ref, hc_ref, wc_ref, bc_ref)
    dt = jax.nn.softplus(dt_ref[...] + dtb_ref[...])
    decay_o[...] = jnp.exp(dt * (-jnp.exp(alog_ref[...])))
    xs_o[...] = xs
    xdt_o[...] = (xs * _expand_heads(dt, xs.shape[1] // head_dim, head_dim)).astype(xdt_o.dtype)


def _ssm_sample_pre(xs, bc, dt, hist, wx, wbc, bx, bbc, dtb, alog, groups, head_dim, n):
    nb, d_ssm = xs.shape
    gw = d_ssm // groups
    hg = gw // head_dim
    cw = wx.shape[0]
    xoff = d_ssm // n
    gcol = lambda g: (0, g)
    in_specs = [
        pl.BlockSpec((nb, gw), gcol),
        pl.BlockSpec((nb, n), gcol),
        pl.BlockSpec((nb, n), lambda g: (0, groups + g)),
        pl.BlockSpec((nb, LANES), gcol),
        pl.BlockSpec((cw - 1, nb, gw), lambda g: (0, 0, g)),
        pl.BlockSpec((cw - 1, nb, n), lambda g: (0, 0, xoff + g)),
        pl.BlockSpec((cw - 1, nb, n), lambda g: (0, 0, xoff + groups + g)),
        pl.BlockSpec((cw, gw), gcol),
        pl.BlockSpec((cw, n), gcol),
        pl.BlockSpec((cw, n), lambda g: (0, groups + g)),
        pl.BlockSpec((1, gw), gcol),
        pl.BlockSpec((1, n), gcol),
        pl.BlockSpec((1, n), lambda g: (0, groups + g)),
        pl.BlockSpec((1, LANES), gcol),
        pl.BlockSpec((1, LANES), gcol),
    ]
    return pl.pallas_call(
        functools.partial(_ssm_sample_pre_body, cw=cw, head_dim=head_dim),
        grid=(groups,),
        in_specs=in_specs,
        out_specs=[pl.BlockSpec((nb, gw), gcol),
                   pl.BlockSpec((nb, gw), gcol),
                   pl.BlockSpec((nb, n), gcol),
                   pl.BlockSpec((nb, n), gcol),
                   pl.BlockSpec((nb, LANES), gcol)],
        out_shape=[jax.ShapeDtypeStruct((nb, d_ssm), F32),
                   jax.ShapeDtypeStruct((nb, d_ssm), BF16),
                   jax.ShapeDtypeStruct((nb, groups * n), F32),
                   jax.ShapeDtypeStruct((nb, groups * n), F32),
                   jax.ShapeDtypeStruct((nb, groups * LANES), F32)],
        compiler_params=_params(("parallel",), 32 * 1024 * 1024),
        name="ssm_sample_pre",
    )(xs, bc, bc, dt, hist, hist, hist, wx, wbc, wbc, bx, bbc, bbc, dtb, alog)


def _ssm_sample_state_body(decay_ref, xdt_t_ref, bg_ref, cg_ref, xs_ref, z_ref, dskip_ref, normg_ref, st_ref,
                           y_ref, nst_ref, *, groups, hg, head_dim):
    b = pl.program_id(0)
    nb = bg_ref.shape[1]
    n = bg_ref.shape[2]
    gw = hg * head_dim
    heads = groups * hg
    is_b = lax.broadcasted_iota(jnp.int32, (nb, n), 0) == b
    for g in range(groups):
        lo = g * gw
        bsel = jnp.where(is_b, bg_ref[g], 0.0).astype(BF16)
        outer = jnp.dot(xdt_t_ref[lo:lo + gw, :], bsel, preferred_element_type=F32)
        for h in range(hg):
            r0 = lo + h * head_dim
            d = decay_ref[b * heads + g * hg + h]
            nst_ref[0, r0:r0 + head_dim, :] = (st_ref[0, r0:r0 + head_dim, :] * d
                                               + outer[h * head_dim:(h + 1) * head_dim, :])
        c_row = jnp.broadcast_to(cg_ref[g, pl.ds(b, 1), :], (SUBLANES, n)).astype(BF16)
        y = lax.dot_general(c_row, nst_ref[0, lo:lo + gw, :].astype(BF16), (((1,), (1,)), ((), ())),
                            preferred_element_type=F32)[0:1, :]
        y = _group_rmsnorm_gate(y, xs_ref[pl.ds(b, 1), lo:lo + gw], z_ref[pl.ds(b, 1), lo:lo + gw],
                                dskip_ref[:, lo:lo + gw], normg_ref[:, lo:lo + gw])
        y_ref[0, :, lo:lo + gw] = y.astype(y_ref.dtype)


def _ssm_sample_state(decay, xdt_t, bg, cg, xs, z, dskip, normg, state, groups, head_dim):
    nb, d_ssm = xs.shape
    n = state.shape[-1]
    hg = d_ssm // groups // head_dim
    full2 = lambda b: (0, 0)
    full3 = lambda b: (0, 0, 0)
    return pl.pallas_call(
        functools.partial(_ssm_sample_state_body, groups=groups, hg=hg, head_dim=head_dim),
        grid=(nb,),
        in_specs=[pl.BlockSpec(memory_space=pltpu.SMEM),
                  pl.BlockSpec((d_ssm, nb), full2),
                  pl.BlockSpec((groups, nb, n), full3),
                  pl.BlockSpec((groups, nb, n), full3),
                  pl.BlockSpec((nb, d_ssm), full2),
                  pl.BlockSpec((nb, d_ssm), full2),
                  pl.BlockSpec((1, d_ssm), full2),
                  pl.BlockSpec((1, d_ssm), full2),
                  pl.BlockSpec((1, d_ssm, n), lambda b: (b, 0, 0))],
        out_specs=[pl.BlockSpec((1, 1, d_ssm), lambda b: (b, 0, 0)),
                   pl.BlockSpec((1, d_ssm, n), lambda b: (b, 0, 0))],
        out_shape=[jax.ShapeDtypeStruct((nb, 1, d_ssm), F32),
                   jax.ShapeDtypeStruct((nb, d_ssm, n), F32)],
        compiler_params=_params(("arbitrary",), 48 * 1024 * 1024),
        name="ssm_sample_state",
    )(decay, xdt_t, bg, cg, xs, z, dskip, normg, state)


def _softmax_rows(s):
    e = jnp.exp(s - jnp.max(s, axis=-1, keepdims=True))
    return e / jnp.sum(e, axis=-1, keepdims=True)


def _attn_prompt_body(q_ref, k_ref, v_ref, o_ref, *, scale):
    s = lax.dot_general(q_ref[...], k_ref[...].astype(BF16), (((1,), (1,)), ((), ())),
                        preferred_element_type=F32) * scale
    p = _softmax_rows(s)
    o_ref[...] = jnp.dot(p.astype(BF16), v_ref[...].astype(BF16), preferred_element_type=F32).astype(o_ref.dtype)


def _attn_prompt(q, k, v, batch, seqlen, n_mem, heads):
    m, d = q.shape
    hd = d // heads
    tq = min(512, seqlen)
    nq = seqlen // tq
    return pl.pallas_call(
        functools.partial(_attn_prompt_body, scale=hd ** -0.5),
        grid=(batch, heads, nq),
        in_specs=[pl.BlockSpec((tq, hd), lambda b, h, i: (b * nq + i, h)),
                  pl.BlockSpec((n_mem, hd), lambda b, h, i: (b, h)),
                  pl.BlockSpec((n_mem, hd), lambda b, h, i: (b, h))],
        out_specs=pl.BlockSpec((tq, hd), lambda b, h, i: (b * nq + i, h)),
        out_shape=jax.ShapeDtypeStruct((m, d), BF16),
        compiler_params=_params(("parallel", "parallel", "parallel"), 32 * 1024 * 1024),
        name="attn_prompt",
    )(q, k, v)


def _attn_sample_body(q_ref, k_ref, v_ref, o_ref, *, scale, chunk):
    q = q_ref[0]
    n_mem = k_ref.shape[2]
    s = jnp.concatenate(
        [jnp.sum(k_ref[0, 0, m0:m0 + chunk] * q[None], axis=-1, keepdims=True) for m0 in range(0, n_mem, chunk)],
        axis=0) * scale
    e = jnp.exp(s - jnp.max(s, axis=0, keepdims=True))
    p = e / jnp.sum(e, axis=0, keepdims=True)
    o = jnp.sum(p[0:chunk] * v_ref[0, 0, 0:chunk], axis=0)
    for m0 in range(chunk, n_mem, chunk):
        o = o + jnp.sum(p[m0:m0 + chunk] * v_ref[0, 0, m0:m0 + chunk], axis=0)
    o_ref[0] = o


def _attn_sample(q, k, v):
    _, nb, n_mem, heads, hd = k.shape
    chunk = min(32, n_mem)
    assert n_mem % chunk == 0
    kv_spec = pl.BlockSpec((1, 1, n_mem, heads, hd), lambda b: (0, b, 0, 0, 0))
    return pl.pallas_call(
        functools.partial(_attn_sample_body, scale=hd ** -0.5, chunk=chunk),
        grid=(nb,),
        in_specs=[pl.BlockSpec((1, heads, hd), lambda b: (b, 0, 0)), kv_spec, kv_spec],
        out_specs=pl.BlockSpec((1, heads, hd), lambda b: (b, 0, 0)),
        out_shape=jax.ShapeDtypeStruct((nb, heads, hd), F32),
        compiler_params=_params(("parallel",), 4 * n_mem * SUBLANES * hd * 4 + 8 * 1024 * 1024),
        name="attn_sample",
    )(q, k, v)


def kernel(x_prompt, x_sample, mem_prompt, state_conv, state_ssm_conv, state_ssm, cache_mem_k, cache_mem_v,
           g_mix, w_in, conv_w, conv_b, ln_g, ln_b, ssm_conv_w, ssm_conv_b, dt_bias, a_log, d_skip,
           ssm_norm_g, w_out, g_xattn, g_mem, w_q, w_k, w_v, w_o, g_mlp, w_up, w_down, g_final):
    depth = g_mix.shape[0]
    assert depth == 1
    batch, seqlen, d = x_prompt.shape
    nb = x_sample.shape[0]
    assert x_sample.shape[1] == 1
    d_conv = conv_w.shape[-1]
    c_xbc = ssm_conv_w.shape[-1]
    d_ssm = ssm_norm_g.shape[-1]
    heads = dt_bias.shape[-1]
    head_dim = d_ssm // heads
    n = state_ssm.shape[-1]
    groups = (c_xbc - d_ssm) // (2 * n)
    hg = heads // groups
    n_mem, xa_heads = cache_mem_k.shape[2], cache_mem_k.shape[3]
    cw = ssm_conv_w.shape[1]
    assert n == LANES and hg <= LANES

    o_z, o_x = 2 * d_conv, 2 * d_conv + d_ssm
    o_b, o_dt = o_x + d_ssm, o_x + c_xbc
    w_in_t = jnp.swapaxes(w_in, 1, 2)
    w_dt_t = jnp.pad(w_in_t[0, o_dt:].reshape(groups, hg, d), ((0, 0), (0, LANES - hg), (0, 0)))
    w_dt_t = w_dt_t.reshape(1, groups * LANES, d)
    wx, wbc = ssm_conv_w[0][:, :d_ssm], ssm_conv_w[0][:, d_ssm:]
    bx, bbc = ssm_conv_b[0][:d_ssm].reshape(1, d_ssm), ssm_conv_b[0][d_ssm:].reshape(1, c_xbc - d_ssm)
    dtb, alog = _pad_heads(dt_bias[0], groups), _pad_heads(a_log[0], groups)
    dskip = jnp.repeat(d_skip[0].astype(F32), head_dim).reshape(1, d_ssm)
    normg = ssm_norm_g[0].reshape(1, d_ssm)

    mp = batch * seqlen
    xp, xsm = x_prompt.reshape(mp, d), x_sample.reshape(nb, d)
    norm2 = lambda a_p, a_s, g, dtype: (_rmsnorm(a_p, g, dtype), _rmsnorm(a_s, g, dtype))
    in_proj = functools.partial(_proj, *norm2(xp, xsm, g_mix[0], BF16), out_dtype=F32, w_rows_are_outputs=True)
    glu, glu_s = in_proj(w_in_t, col0=0, ncols=o_z)
    z, z_s = in_proj(w_in_t, col0=o_z, ncols=d_ssm)
    xs, xs_s = in_proj(w_in_t, col0=o_x, ncols=d_ssm)
    bc, bc_s = in_proj(w_in_t, col0=o_b, ncols=o_dt - o_b)
    dt, dt_s = in_proj(w_dt_t)

    v_p, conv_hist = _conv_prompt(glu, conv_w[0], conv_b[0], ln_g[0], ln_b[0], batch, seqlen)
    y_p, st_p = _ssd_prompt(xs, z, bc, dt, wx, wbc, bx, bbc, dtb, alog, dskip, normg,
                            batch, seqlen, groups, head_dim, n)
    keep = conv_w.shape[1] - 1
    new_conv_p = conv_hist[:, CONV_HALO - keep:, :]
    tail = lambda a: a.reshape(batch, seqlen, -1)[:, seqlen - (cw - 1):, :]
    new_ssm_conv_p = jnp.concatenate([tail(xs), tail(bc)], axis=-1)

    v_s, new_conv_t = _conv_sample(glu_s, jnp.swapaxes(state_conv[0], 0, 1), conv_w[0], conv_b[0], ln_g[0], ln_b[0])
    new_conv_s = jnp.swapaxes(new_conv_t, 0, 1)[None]
    hist = jnp.swapaxes(state_ssm_conv[0], 0, 1)
    xs_c, xdt, b_c, c_c, decay = _ssm_sample_pre(xs_s, bc_s, dt_s, hist, wx, wbc, bx, bbc, dtb, alog,
                                                  groups, head_dim, n)
    decay_flat = decay.reshape(nb, groups, LANES)[:, :, :hg].reshape(nb * heads)
    to_groups = lambda a: jnp.swapaxes(a.reshape(nb, groups, n), 0, 1)
    y_s, st_s = _ssm_sample_state(decay_flat, xdt.T, to_groups(b_c), to_groups(c_c), xs_c, z_s, dskip, normg,
                                  state_ssm[0].reshape(nb, d_ssm, n), groups, head_dim)
    new_ssm_conv_s = jnp.swapaxes(
        jnp.concatenate([hist[1:], jnp.concatenate([xs_s, bc_s], axis=-1)[None]], axis=0), 0, 1)

    x1, x1_s = _kgrid([v_p, y_p], [v_s, y_s.reshape(nb, d_ssm).astype(BF16)], w_out[0].astype(BF16),
                      res_p=xp, res_s=xsm)
    q, q_s = _proj(*norm2(x1, x1_s, g_xattn[0], BF16), w_q, out_dtype=BF16)
    mem_n = _rmsnorm(mem_prompt.reshape(batch * n_mem, d), g_mem[0], BF16)
    k_p, _ = _proj(mem_n, None, w_k, out_dtype=F32)
    v_mem_p, _ = _proj(mem_n, None, w_v, out_dtype=F32)
    o = _attn_prompt(q, k_p, v_mem_p, batch, seqlen, n_mem, xa_heads)
    o_s = _attn_sample(q_s.astype(F32).reshape(nb, xa_heads, d // xa_heads), cache_mem_k,
                       cache_mem_v).reshape(nb, d).astype(BF16)
    x2, x2_s = _proj(o, o_s, w_o, out_dtype=F32, res_p=x1, res_s=x1_s)
    up, up_s = _proj(*norm2(x2, x2_s, g_mlp[0], BF16), w_up, out_dtype=BF16, act="relu2")
    x3, x3_s = _kgrid([up], [up_s], w_down[0].astype(BF16), res_p=x2, res_s=x2_s)
    y_prompt, y_sample = norm2(x3, x3_s, g_final, F32)

    kv_shape = (1, batch, n_mem, xa_heads, d // xa_heads)
    return (y_prompt.reshape(batch, seqlen, d), y_sample.reshape(nb, 1, d),
            new_conv_p[None], new_ssm_conv_p[None], st_p.reshape(1, batch, heads, head_dim, n),
            k_p.reshape(kv_shape), v_mem_p.reshape(kv_shape),
            new_conv_s, new_ssm_conv_s[None], st_s.reshape(1, nb, heads, head_dim, n))
```

```python
import functools

import jax
import jax.numpy as jnp
from jax import lax
from jax.experimental import pallas as pl
from jax.experimental.pallas import tpu as pltpu

F32 = jnp.float32
BF16 = jnp.bfloat16
EPS = 1e-5
LANES = 128
SUBLANES = 8
MXU_DIM = 256
VMEM_CAP = 62 * 1024 * 1024
PROJ_VMEM_SLACK = 4 * 1024 * 1024


def _params(sem, vmem_bytes):
    return pltpu.CompilerParams(dimension_semantics=sem,
                                vmem_limit_bytes=int(min(VMEM_CAP, max(vmem_bytes, 16 * 1024 * 1024))))


def _sigmoid(x):
    return 0.5 * jnp.tanh(0.5 * x) + 0.5


def _silu(x):
    h = 0.5 * x
    return h * jnp.tanh(h) + h


def _split3(x):
    hi = x.astype(BF16)
    r1 = x - hi.astype(F32)
    mid = r1.astype(BF16)
    return hi, mid, (r1 - mid.astype(F32)).astype(BF16)


def _tile(dim, cap, unit):
    if dim <= cap:
        return dim
    best = max(t for t in range(unit, cap + 1, unit) if dim % t == 0)
    return best


def _rmsnorm_body(x_ref, g_ref, o_ref):
    x = x_ref[...]
    ms = jnp.mean(x * x, axis=-1, keepdims=True)
    o_ref[...] = (x * lax.rsqrt(ms + EPS) * g_ref[...]).astype(o_ref.dtype)


def _rmsnorm(x, g, out_dtype, bm=256):
    m, d = x.shape
    bm = min(bm, m)
    return pl.pallas_call(
        _rmsnorm_body,
        grid=(m // bm,),
        in_specs=[pl.BlockSpec((bm, d), lambda i: (i, 0)),
                  pl.BlockSpec((1, d), lambda i: (0, 0))],
        out_specs=pl.BlockSpec((bm, d), lambda i: (i, 0)),
        out_shape=jax.ShapeDtypeStruct((m, d), out_dtype),
        compiler_params=_params(("parallel",), 6 * bm * d * 4),
        name="rmsnorm",
    )(x, g.reshape(1, d).astype(F32))


def _epilogue(acc, res_ref, o_ref, act):
    if act == "relu2":
        acc = jnp.square(jnp.maximum(acc, 0.0))
    if res_ref is not None:
        acc = res_ref[...] + acc
    o_ref[...] = acc.astype(o_ref.dtype)


def _proj_body(*refs, act, has_s, has_res, w_rows_are_outputs):
    if w_rows_are_outputs:
        mm = lambda a, w: lax.dot_general(a, w, (((1,), (1,)), ((), ())), preferred_element_type=F32)
    else:
        mm = lambda a, w: jnp.dot(a, w, preferred_element_type=F32)
    it = iter(refs)
    ap_ref = next(it)
    as_ref = next(it) if has_s else None
    w_ref = next(it)
    rp_ref = next(it) if has_res else None
    rs_ref = next(it) if has_res and has_s else None
    op_ref = next(it)
    os_ref = next(it) if has_s else None
    w16 = next(it)

    @pl.when(pl.program_id(1) == 0)
    def _():
        w16[...] = w_ref[...].astype(BF16)
        if has_s:
            _epilogue(mm(as_ref[...], w16[...]), rs_ref, os_ref, act)

    _epilogue(mm(ap_ref[...], w16[...]), rp_ref, op_ref, act)


def _proj(a_p, a_s, w, *, out_dtype, col0=0, ncols=None, act=None, res_p=None, res_s=None, w_rows_are_outputs=False):
    mp, kdim = a_p.shape
    ncols = w.shape[1 if w_rows_are_outputs else 2] - col0 if ncols is None else ncols
    has_s, has_res = a_s is not None, res_p is not None
    osz = jnp.dtype(out_dtype).itemsize
    ms = a_s.shape[0] if has_s else 0

    def footprint(bm, bn):
        return (2 * kdim * bn * 4 + kdim * bn * 2 + 2 * (bm + ms) * kdim * 2
                + (2 * osz + 2 * 4 * has_res + 4) * (bm + ms) * bn)

    for bm_cap, bn_cap in ((512, 1024), (1024, 512)):
        bm = _tile(mp, bm_cap, SUBLANES)
        bn = max(t for t in range(LANES, bn_cap + 1, LANES) if col0 % t == 0 and ncols % t == 0)
        if footprint(bm, bn) + PROJ_VMEM_SLACK <= VMEM_CAP:
            break
    c0 = col0 // bn
    vmem = footprint(bm, bn) + PROJ_VMEM_SLACK
    in_specs = [pl.BlockSpec((bm, kdim), lambda j, i: (i, 0))]
    args = [a_p]
    out_specs = [pl.BlockSpec((bm, bn), lambda j, i: (i, j))]
    out_shape = [jax.ShapeDtypeStruct((mp, ncols), out_dtype)]
    if has_s:
        in_specs.append(pl.BlockSpec((ms, kdim), lambda j, i: (0, 0)))
        args.append(a_s)
        out_specs.append(pl.BlockSpec((ms, bn), lambda j, i: (0, j)))
        out_shape.append(jax.ShapeDtypeStruct((ms, ncols), out_dtype))
    if w_rows_are_outputs:
        in_specs.append(pl.BlockSpec((None, bn, kdim), lambda j, i: (0, c0 + j, 0)))
    else:
        in_specs.append(pl.BlockSpec((None, kdim, bn), lambda j, i: (0, 0, c0 + j)))
    args.append(w)
    if has_res:
        in_specs.append(pl.BlockSpec((bm, bn), lambda j, i: (i, j)))
        args.append(res_p)
        if has_s:
            in_specs.append(pl.BlockSpec((ms, bn), lambda j, i: (0, j)))
            args.append(res_s)
    out = pl.pallas_call(
        functools.partial(_proj_body, act=act, has_s=has_s, has_res=has_res, w_rows_are_outputs=w_rows_are_outputs),
        grid=(ncols // bn, mp // bm),
        in_specs=in_specs,
        out_specs=out_specs,
        out_shape=out_shape,
        scratch_shapes=[pltpu.VMEM((bn, kdim) if w_rows_are_outputs else (kdim, bn), BF16)],
        compiler_params=_params(("parallel", "arbitrary"), vmem),
        name="proj",
    )(*args)
    return (out[0], out[1]) if has_s else (out[0], None)


def _kgrid_body(*refs, part_blocks, has_res):
    nparts = len(part_blocks)
    nk = sum(part_blocks)
    it = iter(refs)
    ap_refs = [next(it) for _ in range(nparts)]
    as_refs = [next(it) for _ in range(nparts)]
    w_ref = next(it)
    rp_ref = next(it) if has_res else None
    rs_ref = next(it) if has_res else None
    op_ref, os_ref, accp_ref, accs_ref = next(it), next(it), next(it), next(it)
    i, k = pl.program_id(1), pl.program_id(2)

    def reduce_k(a_refs, acc_ref, res_ref, o_ref):
        lo = 0
        for p, blocks in enumerate(part_blocks):
            hi = lo + blocks
            first, last = max(lo, 1), min(hi, nk - 1)
            dot = lambda p=p: jnp.dot(a_refs[p][...], w_ref[...], preferred_element_type=F32)
            if lo == 0:
                @pl.when(k == 0)
                def _(dot=dot):
                    acc_ref[...] = dot()
            if last > first:
                @pl.when((k >= first) & (k < last))
                def _(dot=dot):
                    acc_ref[...] += dot()
            if hi == nk:
                @pl.when(k == nk - 1)
                def _(dot=dot):
                    _epilogue(acc_ref[...] + dot(), res_ref, o_ref, None)
            lo = hi

    reduce_k(ap_refs, accp_ref, rp_ref, op_ref)

    @pl.when(i == 0)
    def _():
        reduce_k(as_refs, accs_ref, rs_ref, os_ref)


def _kgrid(parts_p, parts_s, w, *, res_p=None, res_s=None):
    mp, ms = parts_p[0].shape[0], parts_s[0].shape[0]
    kdim, n = w.shape
    assert sum(a.shape[1] for a in parts_p) == kdim
    bm = _tile(mp, 1024, SUBLANES)
    bn = _tile(n, 1024, LANES)
    bk = 2048
    while any(a.shape[1] % bk for a in parts_p):
        bk //= 2
    part_blocks = tuple(a.shape[1] // bk for a in parts_p)
    nk = sum(part_blocks)
    assert nk >= 2 and bk % LANES == 0
    has_res = res_p is not None
    vmem = (2 * len(parts_p) * (bm + ms) * bk * 2 + 2 * bk * bn * 2 + (5 + 2 * has_res) * (bm + ms) * bn * 4)
    lhs_specs = lambda rows, row_of: [
        pl.BlockSpec((rows, bk), lambda j, i, k, lo=lo, blocks=blocks: (row_of(i), jnp.clip(k - lo, 0, blocks - 1)))
        for lo, blocks in zip([sum(part_blocks[:p]) for p in range(len(part_blocks))], part_blocks)]
    in_specs = lhs_specs(bm, lambda i: i) + lhs_specs(ms, lambda i: 0)
    in_specs.append(pl.BlockSpec((bk, bn), lambda j, i, k: (k, j)))
    args = [*parts_p, *parts_s, w]
    if has_res:
        in_specs += [pl.BlockSpec((bm, bn), lambda j, i, k: (i, j)), pl.BlockSpec((ms, bn), lambda j, i, k: (0, j))]
        args += [res_p, res_s]
    return pl.pallas_call(
        functools.partial(_kgrid_body, part_blocks=part_blocks, has_res=has_res),
        grid=(n // bn, mp // bm, nk),
        in_specs=in_specs,
        out_specs=[pl.BlockSpec((bm, bn), lambda j, i, k: (i, j)), pl.BlockSpec((ms, bn), lambda j, i, k: (0, j))],
        out_shape=[jax.ShapeDtypeStruct((mp, n), F32), jax.ShapeDtypeStruct((ms, n), F32)],
        scratch_shapes=[pltpu.VMEM((bm, bn), F32), pltpu.VMEM((ms, bn), F32)],
        compiler_params=_params(("parallel", "arbitrary", "arbitrary"), vmem + 2 * 1024 * 1024),
        name="kgrid",
    )(*args)


CONV_HALO = 32
CONV_ROWS = 32
CONV_STRIP = 512


def _conv_prompt_body(a_ref, g_ref, w_ref, b_ref, lng_ref, lnb_ref, v_ref, hist_ref, ubuf, sbuf, cbuf,
                      *, tile, width, nt):
    t = pl.program_id(1)
    ch = ubuf.shape[1]

    @pl.when(t == 0)
    def _():
        ubuf[0:CONV_HALO, :] = jnp.zeros((CONV_HALO, ch), F32)

    ubuf[CONV_HALO:CONV_HALO + tile, :] = a_ref[...] * _sigmoid(g_ref[...])
    off0 = CONV_HALO - (width - 1)
    srows = CONV_HALO + tile - SUBLANES

    def strip(c, carry):
        col = pl.ds(pl.multiple_of(c * CONV_STRIP, CONV_STRIP), CONV_STRIP)
        for r in range(1, SUBLANES):
            sbuf[r, 0:srows, :] = ubuf[pl.ds(r, srows), col]
        for r0 in range(0, tile, CONV_ROWS):
            acc = jnp.broadcast_to(b_ref[:, col], (CONV_ROWS, CONV_STRIP))
            for k in range(width):
                whole, r = divmod(off0 + k, SUBLANES)
                if r == 0:
                    rows = ubuf[pl.ds(r0 + off0 + k, CONV_ROWS), col]
                else:
                    rows = sbuf[r, pl.ds(r0 + whole * SUBLANES, CONV_ROWS), :]
                acc = acc + w_ref[k:k + 1, col] * rows
            cbuf[r0:r0 + CONV_ROWS, col] = acc
        return carry

    lax.fori_loop(0, ch // CONV_STRIP, strip, 0)

    y = cbuf[...]
    yc = y - jnp.mean(y, axis=-1, keepdims=True)
    yn = yc * lax.rsqrt(jnp.mean(yc * yc, axis=-1, keepdims=True) + EPS)
    v_ref[...] = _silu(yn * lng_ref[...] + lnb_ref[...]).astype(v_ref.dtype)

    ubuf[0:CONV_HALO, :] = ubuf[tile:tile + CONV_HALO, :]

    @pl.when(t == nt - 1)
    def _():
        hist_ref[0] = ubuf[0:CONV_HALO, :]


def _conv_prompt(glu, conv_w, conv_b, ln_g, ln_b, batch, seqlen):
    width, ch = conv_w.shape
    tile = min(128, seqlen)
    nt = seqlen // tile
    assert width - 1 <= CONV_HALO <= tile and seqlen % tile == 0
    assert ch % CONV_STRIP == 0 and tile % CONV_ROWS == 0
    row = lambda b, t: (b * nt + t, 0)
    vec = pl.BlockSpec((1, ch), lambda b, t: (0, 0))
    return pl.pallas_call(
        functools.partial(_conv_prompt_body, tile=tile, width=width, nt=nt),
        grid=(batch, nt),
        in_specs=[pl.BlockSpec((tile, ch), row),
                  pl.BlockSpec((tile, ch), lambda b, t: (b * nt + t, 1)),
                  pl.BlockSpec((width, ch), lambda b, t: (0, 0)),
                  vec, vec, vec],
        out_specs=[pl.BlockSpec((tile, ch), row),
                   pl.BlockSpec((1, CONV_HALO, ch), lambda b, t: (b, 0, 0))],
        out_shape=[jax.ShapeDtypeStruct((batch * seqlen, ch), BF16),
                   jax.ShapeDtypeStruct((batch, CONV_HALO, ch), F32)],
        scratch_shapes=[pltpu.VMEM((CONV_HALO + tile, ch), F32),
                        pltpu.VMEM((SUBLANES, CONV_HALO + tile, CONV_STRIP), F32),
                        pltpu.VMEM((tile, ch), F32)],
        compiler_params=_params(("arbitrary", "arbitrary"), 32 * 1024 * 1024),
        name="conv_prompt",
    )(glu, glu, conv_w, conv_b.reshape(1, ch), ln_g.reshape(1, ch), ln_b.reshape(1, ch))


def _conv_sample_body(a_ref, g_ref, st_ref, w_ref, b_ref, lng_ref, lnb_ref, v_ref, nst_ref, *, width):
    u = a_ref[...] * _sigmoid(g_ref[...])
    y = b_ref[...] + w_ref[width - 1:width, :] * u
    for k in range(width - 1):
        y = y + w_ref[k:k + 1, :] * st_ref[k]
    yc = y - jnp.mean(y, axis=-1, keepdims=True)
    yn = yc * lax.rsqrt(jnp.mean(yc * yc, axis=-1, keepdims=True) + EPS)
    v_ref[...] = _silu(yn * lng_ref[...] + lnb_ref[...]).astype(v_ref.dtype)
    for k in range(width - 2):
        nst_ref[k] = st_ref[k + 1]
    nst_ref[width - 2] = u


def _conv_sample(glu, state, conv_w, conv_b, ln_g, ln_b):
    width, ch = conv_w.shape
    nb = glu.shape[0]
    bs = _tile(nb, 2 * SUBLANES, SUBLANES)
    assert nb % bs == 0
    vec = pl.BlockSpec((1, ch), lambda i: (0, 0))
    return pl.pallas_call(
        functools.partial(_conv_sample_body, width=width),
        grid=(nb // bs,),
        in_specs=[pl.BlockSpec((bs, ch), lambda i: (i, 0)),
                  pl.BlockSpec((bs, ch), lambda i: (i, 1)),
                  pl.BlockSpec((width - 1, bs, ch), lambda i: (0, i, 0)),
                  pl.BlockSpec((width, ch), lambda i: (0, 0)),
                  vec, vec, vec],
        out_specs=[pl.BlockSpec((bs, ch), lambda i: (i, 0)),
                   pl.BlockSpec((width - 1, bs, ch), lambda i: (0, i, 0))],
        out_shape=[jax.ShapeDtypeStruct((nb, ch), BF16),
                   jax.ShapeDtypeStruct(state.shape, F32)],
        compiler_params=_params(("parallel",), 32 * 1024 * 1024),
        name="conv_sample",
    )(glu, glu, state, conv_w, conv_b.reshape(1, ch), ln_g.reshape(1, ch), ln_b.reshape(1, ch))


def _expand_heads(v, heads, head_dim):
    rows = v.shape[0]
    per_tile = LANES // head_dim
    lane = lax.broadcasted_iota(jnp.int32, (rows, LANES), 1)
    tiles = []
    for j in range(heads // per_tile):
        h0 = j * per_tile
        t = jnp.broadcast_to(v[:, h0:h0 + 1], (rows, LANES))
        for i in range(1, per_tile):
            t = jnp.where(lane >= i * head_dim, jnp.broadcast_to(v[:, h0 + i:h0 + i + 1], (rows, LANES)), t)
        tiles.append(t)
    return jnp.concatenate(tiles, axis=1)


def _pad_heads(v, groups):
    hg = v.shape[0] // groups
    return jnp.pad(v.reshape(groups, hg).astype(F32), ((0, 0), (0, LANES - hg))).reshape(1, groups * LANES)


def _group_rmsnorm_gate(y, xs, z, dskip, normg):
    y = (y + dskip * xs) * _silu(z)
    return y * lax.rsqrt(jnp.mean(y * y, axis=-1, keepdims=True) + EPS) * normg


def _ssd_prompt_body(xs_ref, z_ref, b_ref, c_ref, dt_ref, wx_ref, wb_ref, wc_ref, bx_ref, bb_ref, bc_ref,
                     dtb_ref, alog_ref, dskip_ref, normg_ref, bd_ref, rep_ref,
                     y_ref, st_ref, state, xbuf, bbuf, cbuf, *, q, nc, head_dim, cw):
    c = pl.program_id(2)
    gw = xbuf.shape[1]
    n = bbuf.shape[1]
    hg = gw // head_dim
    halo = SUBLANES

    @pl.when(c == 0)
    def _():
        state[...] = jnp.zeros(state.shape, F32)
        xbuf[0:halo, :] = jnp.zeros((halo, gw), F32)
        bbuf[0:halo, :] = jnp.zeros((halo, n), F32)
        cbuf[0:halo, :] = jnp.zeros((halo, n), F32)

    def causal_conv(raw_ref, buf, w_ref, bias_ref):
        buf[halo:halo + q, :] = raw_ref[...]
        rows = buf[...]
        acc = bias_ref[...] + w_ref[cw - 1:cw, :] * rows[halo:]
        for j in range(1, cw):
            acc = acc + w_ref[cw - 1 - j:cw - j, :] * pltpu.roll(rows, j, axis=0)[halo:]
        buf[0:halo, :] = rows[q:q + halo]
        return _silu(acc)

    xs = causal_conv(xs_ref, xbuf, wx_ref, bx_ref)
    bm = causal_conv(b_ref, bbuf, wb_ref, bb_ref)
    cm = causal_conv(c_ref, cbuf, wc_ref, bc_ref)

    dt = jax.nn.softplus(dt_ref[...] + dtb_ref[...])
    dta = dt * (-jnp.exp(alog_ref[...]))
    qi = lax.broadcasted_iota(jnp.int32, (q, q), 0)
    si = lax.broadcasted_iota(jnp.int32, (q, q), 1)
    causal = qi >= si
    tri = causal.astype(BF16)
    hi, mid, lo = _split3(dta)
    cs = (jnp.dot(tri, hi, preferred_element_type=F32) + jnp.dot(tri, mid, preferred_element_type=F32)
          + jnp.dot(tri, lo, preferred_element_type=F32))
    cs_t = cs.T
    dt_t = dt.T
    per_head = jnp.concatenate([jnp.exp(cs), dt * jnp.exp(cs[q - 1:q, :] - cs)], axis=0)
    spread = rep_ref[...]
    per_lane = sum(jnp.dot(piece, spread, preferred_element_type=F32) for piece in _split3(per_head))
    exp_cs_x = per_lane[0:q]
    xdt_end = xs * per_lane[q:2 * q]

    xs16 = xs.astype(BF16)
    cm16 = cm.astype(BF16)
    cb = lax.dot_general(cm16, bm.astype(BF16), (((1,), (1,)), ((), ())), preferred_element_type=F32)

    hb = MXU_DIM // head_dim
    slab = hb * head_dim
    cb = jnp.where(causal, cb, 0.0)
    blockdiag = bd_ref[...]
    y_parts = []
    for s in range(hg // hb):
        lhs = []
        for j in range(hb):
            h = s * hb + j
            seg = cs[:, h:h + 1] - cs_t[h:h + 1, :]
            lmat = jnp.exp(jnp.where(causal, seg, 0.0))
            lhs.append((cb * lmat * dt_t[h:h + 1, :]).astype(BF16))
        xslab = xs16[:, s * slab:(s + 1) * slab]
        rhs = jnp.concatenate([xslab] * hb, axis=0) * blockdiag
        y_parts.append(jnp.dot(jnp.concatenate(lhs, axis=1), rhs, preferred_element_type=F32))
    y = jnp.concatenate(y_parts, axis=1)

    st = state[...]
    y = y + jnp.dot(cm16, st.astype(BF16), preferred_element_type=F32) * exp_cs_x
    upd = jnp.dot(bm.T.astype(BF16), xdt_end.astype(BF16), preferred_element_type=F32)
    new_state = st * exp_cs_x[q - 1:q, :] + upd
    state[...] = new_state

    y_ref[...] = _group_rmsnorm_gate(y, xs, z_ref[...], dskip_ref[...], normg_ref[...]).astype(y_ref.dtype)

    @pl.when(c == nc - 1)
    def _():
        st_ref[0] = new_state.T


def _ssd_prompt(xs, z, bc, dt, wx, wbc, bx, bbc, dtb, alog, dskip, normg, batch, seqlen, groups, head_dim, n):
    m, d_ssm = xs.shape
    gw = d_ssm // groups
    hg = gw // head_dim
    cw = wx.shape[0]
    q = min(128, seqlen)
    nc = seqlen // q
    hb = MXU_DIM // head_dim
    assert seqlen % q == 0 and q % LANES == 0 and n == LANES and hg <= LANES
    assert gw % MXU_DIM == 0 and MXU_DIM % head_dim == 0 and cw - 1 <= SUBLANES
    row = lambda b, g, c: (b * nc + c, g)
    gcol = lambda b, g, c: (0, g)
    in_specs = [
        pl.BlockSpec((q, gw), row),
        pl.BlockSpec((q, gw), row),
        pl.BlockSpec((q, n), row),
        pl.BlockSpec((q, n), lambda b, g, c: (b * nc + c, groups + g)),
        pl.BlockSpec((q, LANES), row),
        pl.BlockSpec((cw, gw), gcol),
        pl.BlockSpec((cw, n), gcol),
        pl.BlockSpec((cw, n), lambda b, g, c: (0, groups + g)),
        pl.BlockSpec((1, gw), gcol),
        pl.BlockSpec((1, n), gcol),
        pl.BlockSpec((1, n), lambda b, g, c: (0, groups + g)),
        pl.BlockSpec((1, LANES), gcol),
        pl.BlockSpec((1, LANES), gcol),
        pl.BlockSpec((1, gw), gcol),
        pl.BlockSpec((1, gw), gcol),
        pl.BlockSpec((hb * q, MXU_DIM), lambda b, g, c: (0, 0)),
        pl.BlockSpec((LANES, gw), lambda b, g, c: (0, 0)),
    ]
    spread = (jnp.arange(LANES)[:, None] == (jnp.arange(gw)[None, :] // head_dim)).astype(BF16)
    blockdiag = ((jnp.arange(hb * q)[:, None] // q) == (jnp.arange(MXU_DIM)[None, :] // head_dim)).astype(BF16)
    return pl.pallas_call(
        functools.partial(_ssd_prompt_body, q=q, nc=nc, head_dim=head_dim, cw=cw),
        grid=(batch, groups, nc),
        in_specs=in_specs,
        out_specs=[pl.BlockSpec((q, gw), row),
                   pl.BlockSpec((1, gw, n), lambda b, g, c: (b, g, 0))],
        out_shape=[jax.ShapeDtypeStruct((m, d_ssm), BF16),
                   jax.ShapeDtypeStruct((batch, d_ssm, n), F32)],
        scratch_shapes=[pltpu.VMEM((n, gw), F32),
                        pltpu.VMEM((SUBLANES + q, gw), F32),
                        pltpu.VMEM((SUBLANES + q, n), F32),
                        pltpu.VMEM((SUBLANES + q, n), F32)],
        compiler_params=_params(("arbitrary", "arbitrary", "arbitrary"), 32 * 1024 * 1024),
        name="ssd_prompt",
    )(xs, z, bc, bc, dt, wx, wbc, wbc, bx, bbc, bbc, dtb, alog, dskip, normg, blockdiag, spread)


def _ssm_sample_pre_body(xs_ref, b_ref, c_ref, dt_ref, hx_ref, hb_ref, hc_ref, wx_ref, wb_ref, wc_ref,
                         bx_ref, bb_ref, bc_ref, dtb_ref, alog_ref,
                         xs_o, xdt_o, b_o, c_o, decay_o, *, cw, head_dim):
    def conv(new_ref, hist_ref, w_ref, bias_ref):
        acc = bias_ref[...] + w_ref[cw - 1:cw, :] * new_ref[...]
        for k in range(cw - 1):
            acc = acc + w_ref[k:k + 1, :] * hist_ref[k]
        return _silu(acc)

    xs = conv(xs_ref, hx_ref, wx_ref, bx_ref)
    b_o[...] = conv(b_ref, hb_ref, wb_ref, bb_ref)
    c_o[...] = conv(c_ref, hc_ref, wc_ref, bc_ref)
    dt = jax.nn.softplus(dt_ref[...] + dtb_ref[...])
    decay_o[...] = jnp.exp(dt * (-jnp.exp(alog_ref[...])))
    xs_o[...] = xs
    xdt_o[...] = (xs * _expand_heads(dt, xs.shape[1] // head_dim, head_dim)).astype(xdt_o.dtype)


def _ssm_sample_pre(xs, bc, dt, hist, wx, wbc, bx, bbc, dtb, alog, groups, head_dim, n):
    nb, d_ssm = xs.shape
    gw = d_ssm // groups
    hg = gw // head_dim
    cw = wx.shape[0]
    xoff = d_ssm // n
    gcol = lambda g: (0, g)
    in_specs = [
        pl.BlockSpec((nb, gw), gcol),
        pl.BlockSpec((nb, n), gcol),
        pl.BlockSpec((nb, n), lambda g: (0, groups + g)),
        pl.BlockSpec((nb, LANES), gcol),
        pl.BlockSpec((cw - 1, nb, gw), lambda g: (0, 0, g)),
        pl.BlockSpec((cw - 1, nb, n), lambda g: (0, 0, xoff + g)),
        pl.BlockSpec((cw - 1, nb, n), lambda g: (0, 0, xoff + groups + g)),
        pl.BlockSpec((cw, gw), gcol),
        pl.BlockSpec((cw, n), gcol),
        pl.BlockSpec((cw, n), lambda g: (0, groups + g)),
        pl.BlockSpec((1, gw), gcol),
        pl.BlockSpec((1, n), gcol),
        pl.BlockSpec((1, n), lambda g: (0, groups + g)),
        pl.BlockSpec((1, LANES), gcol),
        pl.BlockSpec((1, LANES), gcol),
    ]
    return pl.pallas_call(
        functools.partial(_ssm_sample_pre_body, cw=cw, head_dim=head_dim),
        grid=(groups,),
        in_specs=in_specs,
        out_specs=[pl.BlockSpec((nb, gw), gcol),
                   pl.BlockSpec((nb, gw), gcol),
                   pl.BlockSpec((nb, n), gcol),
                   pl.BlockSpec((nb, n), gcol),
                   pl.BlockSpec((nb, LANES), gcol)],
        out_shape=[jax.ShapeDtypeStruct((nb, d_ssm), F32),
                   jax.ShapeDtypeStruct((nb, d_ssm), BF16),
                   jax.ShapeDtypeStruct((nb, groups * n), F32),
                   jax.ShapeDtypeStruct((nb, groups * n), F32),
                   jax.ShapeDtypeStruct((nb, groups * LANES), F32)],
        compiler_params=_params(("parallel",), 32 * 1024 * 1024),
        name="ssm_sample_pre",
    )(xs, bc, bc, dt, hist, hist, hist, wx, wbc, wbc, bx, bbc, bbc, dtb, alog)


def _ssm_sample_state_body(decay_ref, xdt_t_ref, bg_ref, cg_ref, xs_ref, z_ref, dskip_ref, normg_ref, st_ref,
                           y_ref, nst_ref, *, groups, hg, head_dim):
    b = pl.program_id(0)
    nb = bg_ref.shape[1]
    n = bg_ref.shape[2]
    gw = hg * head_dim
    heads = groups * hg
    is_b = lax.broadcasted_iota(jnp.int32, (nb, n), 0) == b
    for g in range(groups):
        lo = g * gw
        bsel = jnp.where(is_b, bg_ref[g], 0.0).astype(BF16)
        outer = jnp.dot(xdt_t_ref[lo:lo + gw, :], bsel, preferred_element_type=F32)
        for h in range(hg):
            r0 = lo + h * head_dim
            d = decay_ref[b * heads + g * hg + h]
            nst_ref[0, r0:r0 + head_dim, :] = (st_ref[0, r0:r0 + head_dim, :] * d
                                               + outer[h * head_dim:(h + 1) * head_dim, :])
        c_row = jnp.broadcast_to(cg_ref[g, pl.ds(b, 1), :], (SUBLANES, n)).astype(BF16)
        y = lax.dot_general(c_row, nst_ref[0, lo:lo + gw, :].astype(BF16), (((1,), (1,)), ((), ())),
                            preferred_element_type=F32)[0:1, :]
        y = _group_rmsnorm_gate(y, xs_ref[pl.ds(b, 1), lo:lo + gw], z_ref[pl.ds(b, 1), lo:lo + gw],
                                dskip_ref[:, lo:lo + gw], normg_ref[:, lo:lo + gw])
        y_ref[0, :, lo:lo + gw] = y.astype(y_ref.dtype)


def _ssm_sample_state(decay, xdt_t, bg, cg, xs, z, dskip, normg, state, groups, head_dim):
    nb, d_ssm = xs.shape
    n = state.shape[-1]
    hg = d_ssm // groups // head_dim
    full2 = lambda b: (0, 0)
    full3 = lambda b: (0, 0, 0)
    return pl.pallas_call(
        functools.partial(_ssm_sample_state_body, groups=groups, hg=hg, head_dim=head_dim),
        grid=(nb,),
        in_specs=[pl.BlockSpec(memory_space=pltpu.SMEM),
                  pl.BlockSpec((d_ssm, nb), full2),
                  pl.BlockSpec((groups, nb, n), full3),
                  pl.BlockSpec((groups, nb, n), full3),
                  pl.BlockSpec((nb, d_ssm), full2),
                  pl.BlockSpec((nb, d_ssm), full2),
                  pl.BlockSpec((1, d_ssm), full2),
                  pl.BlockSpec((1, d_ssm), full2),
                  pl.BlockSpec((1, d_ssm, n), lambda b: (b, 0, 0))],
        out_specs=[pl.BlockSpec((1, 1, d_ssm), lambda b: (b, 0, 0)),
                   pl.BlockSpec((1, d_ssm, n), lambda b: (b, 0, 0))],
        out_shape=[jax.ShapeDtypeStruct((nb, 1, d_ssm), F32),
                   jax.ShapeDtypeStruct((nb, d_ssm, n), F32)],
        compiler_params=_params(("arbitrary",), 48 * 1024 * 1024),
        name="ssm_sample_state",
    )(decay, xdt_t, bg, cg, xs, z, dskip, normg, state)


def _softmax_rows(s):
    e = jnp.exp(s - jnp.max(s, axis=-1, keepdims=True))
    return e / jnp.sum(e, axis=-1, keepdims=True)


def _attn_prompt_body(q_ref, k_ref, v_ref, o_ref, *, scale):
    s = lax.dot_general(q_ref[...], k_ref[...].astype(BF16), (((1,), (1,)), ((), ())),
                        preferred_element_type=F32) * scale
    p = _softmax_rows(s)
    o_ref[...] = jnp.dot(p.astype(BF16), v_ref[...].astype(BF16), preferred_element_type=F32).astype(o_ref.dtype)


def _attn_prompt(q, k, v, batch, seqlen, n_mem, heads):
    m, d = q.shape
    hd = d // heads
    tq = min(512, seqlen)
    nq = seqlen // tq
    return pl.pallas_call(
        functools.partial(_attn_prompt_body, scale=hd ** -0.5),
        grid=(batch, heads, nq),
        in_specs=[pl.BlockSpec((tq, hd), lambda b, h, i: (b * nq + i, h)),
                  pl.BlockSpec((n_mem, hd), lambda b, h, i: (b, h)),
                  pl.BlockSpec((n_mem, hd), lambda b, h, i: (b, h))],
        out_specs=pl.BlockSpec((tq, hd), lambda b, h, i: (b * nq + i, h)),
        out_shape=jax.ShapeDtypeStruct((m, d), BF16),
        compiler_params=_params(("parallel", "parallel", "parallel"), 32 * 1024 * 1024),
        name="attn_prompt",
    )(q, k, v)


def _attn_sample_body(q_ref, k_ref, v_ref, o_ref, *, scale, chunk):
    q = q_ref[0]
    n_mem = k_ref.shape[2]
    s = jnp.concatenate(
        [jnp.sum(k_ref[0, 0, m0:m0 + chunk] * q[None], axis=-1, keepdims=True) for m0 in range(0, n_mem, chunk)],
        axis=0) * scale
    e = jnp.exp(s - jnp.max(s, axis=0, keepdims=True))
    p = e / jnp.sum(e, axis=0, keepdims=True)
    o = jnp.sum(p[0:chunk] * v_ref[0, 0, 0:chunk], axis=0)
    for m0 in range(chunk, n_mem, chunk):
        o = o + jnp.sum(p[m0:m0 + chunk] * v_ref[0, 0, m0:m0 + chunk], axis=0)
    o_ref[0] = o


def _attn_sample(q, k, v):
    _, nb, n_mem, heads, hd = k.shape
    chunk = min(32, n_mem)
    assert n_mem % chunk == 0
    kv_spec = pl.BlockSpec((1, 1, n_mem, heads, hd), lambda b: (0, b, 0, 0, 0))
    return pl.pallas_call(
        functools.partial(_attn_sample_body, scale=hd ** -0.5, chunk=chunk),
        grid=(nb,),
        in_specs=[pl.BlockSpec((1, heads, hd), lambda b: (b, 0, 0)), kv_spec, kv_spec],
        out_specs=pl.BlockSpec((1, heads, hd), lambda b: (b, 0, 0)),
        out_shape=jax.ShapeDtypeStruct((nb, heads, hd), F32),
        compiler_params=_params(("parallel",), 4 * n_mem * SUBLANES * hd * 4 + 8 * 1024 * 1024),
        name="attn_sample",
    )(q, k, v)


def kernel(x_prompt, x_sample, mem_prompt, state_conv, state_ssm_conv, state_ssm, cache_mem_k, cache_mem_v,
           g_mix, w_in, conv_w, conv_b, ln_g, ln_b, ssm_conv_w, ssm_conv_b, dt_bias, a_log, d_skip,
           ssm_norm_g, w_out, g_xattn, g_mem, w_q, w_k, w_v, w_o, g_mlp, w_up, w_down, g_final):
    depth = g_mix.shape[0]
    assert depth == 1
    batch, seqlen, d = x_prompt.shape
    nb = x_sample.shape[0]
    assert x_sample.shape[1] == 1
    d_conv = conv_w.shape[-1]
    c_xbc = ssm_conv_w.shape[-1]
    d_ssm = ssm_norm_g.shape[-1]
    heads = dt_bias.shape[-1]
    head_dim = d_ssm // heads
    n = state_ssm.shape[-1]
    groups = (c_xbc - d_ssm) // (2 * n)
    hg = heads // groups
    n_mem, xa_heads = cache_mem_k.shape[2], cache_mem_k.shape[3]
    cw = ssm_conv_w.shape[1]
    assert n == LANES and hg <= LANES

    o_z, o_x = 2 * d_conv, 2 * d_conv + d_ssm
    o_b, o_dt = o_x + d_ssm, o_x + c_xbc
    w_in_t = jnp.swapaxes(w_in, 1, 2)
    w_dt_t = jnp.pad(w_in_t[0, o_dt:].reshape(groups, hg, d), ((0, 0), (0, LANES - hg), (0, 0)))
    w_dt_t = w_dt_t.reshape(1, groups * LANES, d)
    wx, wbc = ssm_conv_w[0][:, :d_ssm], ssm_conv_w[0][:, d_ssm:]
    bx, bbc = ssm_conv_b[0][:d_ssm].reshape(1, d_ssm), ssm_conv_b[0][d_ssm:].reshape(1, c_xbc - d_ssm)
    dtb, alog = _pad_heads(dt_bias[0], groups), _pad_heads(a_log[0], groups)
    dskip = jnp.repeat(d_skip[0].astype(F32), head_dim).reshape(1, d_ssm)
    normg = ssm_norm_g[0].reshape(1, d_ssm)

    mp = batch * seqlen
    xp, xsm = x_prompt.reshape(mp, d), x_sample.reshape(nb, d)
    norm2 = lambda a_p, a_s, g, dtype: (_rmsnorm(a_p, g, dtype), _rmsnorm(a_s, g, dtype))
    in_proj = functools.partial(_proj, *norm2(xp, xsm, g_mix[0], BF16), out_dtype=F32, w_rows_are_outputs=True)
    glu, glu_s = in_proj(w_in_t, col0=0, ncols=o_z)
    z, z_s = in_proj(w_in_t, col0=o_z, ncols=d_ssm)
    xs, xs_s = in_proj(w_in_t, col0=o_x, ncols=d_ssm)
    bc, bc_s = in_proj(w_in_t, col0=o_b, ncols=o_dt - o_b)
    dt, dt_s = in_proj(w_dt_t)

    v_p, conv_hist = _conv_prompt(glu, conv_w[0], conv_b[0], ln_g[0], ln_b[0], batch, seqlen)
    y_p, st_p = _ssd_prompt(xs, z, bc, dt, wx, wbc, bx, bbc, dtb, alog, dskip, normg,
                            batch, seqlen, groups, head_dim, n)
    keep = conv_w.shape[1] - 1
    new_conv_p = conv_hist[:, CONV_HALO - keep:, :]
    tail = lambda a: a.reshape(batch, seqlen, -1)[:, seqlen - (cw - 1):, :]
    new_ssm_conv_p = jnp.concatenate([tail(xs), tail(bc)], axis=-1)

    v_s, new_conv_t = _conv_sample(glu_s, jnp.swapaxes(state_conv[0], 0, 1), conv_w[0], conv_b[0], ln_g[0], ln_b[0])
    new_conv_s = jnp.swapaxes(new_conv_t, 0, 1)[None]
    hist = jnp.swapaxes(state_ssm_conv[0], 0, 1)
    xs_c, xdt, b_c, c_c, decay = _ssm_sample_pre(xs_s, bc_s, dt_s, hist, wx, wbc, bx, bbc, dtb, alog,
                                                  groups, head_dim, n)
    decay_flat = decay.reshape(nb, groups, LANES)[:, :, :hg].reshape(nb * heads)
    to_groups = lambda a: jnp.swapaxes(a.reshape(nb, groups, n), 0, 1)
    y_s, st_s = _ssm_sample_state(decay_flat, xdt.T, to_groups(b_c), to_groups(c_c), xs_c, z_s, dskip, normg,
                                  state_ssm[0].reshape(nb, d_ssm, n), groups, head_dim)
    new_ssm_conv_s = jnp.swapaxes(
        jnp.concatenate([hist[1:], jnp.concatenate([xs_s, bc_s], axis=-1)[None]], axis=0), 0, 1)

    x1, x1_s = _kgrid([v_p, y_p], [v_s, y_s.reshape(nb, d_ssm).astype(BF16)], w_out[0].astype(BF16),
                      res_p=xp, res_s=xsm)
    q, q_s = _proj(*norm2(x1, x1_s, g_xattn[0], BF16), w_q, out_dtype=BF16)
    mem_n = _rmsnorm(mem_prompt.reshape(batch * n_mem, d), g_mem[0], BF16)
    k_p, _ = _proj(mem_n, None, w_k, out_dtype=F32)
    v_mem_p, _ = _proj(mem_n, None, w_v, out_dtype=F32)
    o = _attn_prompt(q, k_p, v_mem_p, batch, seqlen, n_mem, xa_heads)
    o_s = _attn_sample(q_s.astype(F32).reshape(nb, xa_heads, d // xa_heads), cache_mem_k,
                       cache_mem_v).reshape(nb, d).astype(BF16)
    x2, x2_s = _proj(o, o_s, w_o, out_dtype=F32, res_p=x1, res_s=x1_s)
    up, up_s = _proj(*norm2(x2, x2_s, g_mlp[0], BF16), w_up, out_dtype=BF16, act="relu2")
    x3, x3_s = _kgrid([up], [up_s], w_down[0].astype(BF16), res_p=x2, res_s=x2_s)
    y_prompt, y_sample = norm2(x3, x3_s, g_final, F32)

    kv_shape = (1, batch, n_mem, xa_heads, d // xa_heads)
    return (y_prompt.reshape(batch, seqlen, d), y_sample.reshape(nb, 1, d),
            new_conv_p[None], new_ssm_conv_p[None], st_p.reshape(1, batch, heads, head_dim, n),
            k_p.reshape(kv_shape), v_mem_p.reshape(kv_shape),
            new_conv_s, new_ssm_conv_s[None], st_s.reshape(1, nb, heads, head_dim, n))
```

```python
import functools

import jax
import jax.numpy as jnp
from jax import lax
from jax.experimental import pallas as pl
from jax.experimental.pallas import tpu as pltpu

F32 = jnp.float32
BF16 = jnp.bfloat16
EPS = 1e-5
LANES = 128
SUBLANES = 8
MXU_DIM = 256
VMEM_CAP = 62 * 1024 * 1024
PROJ_VMEM_SLACK = 4 * 1024 * 1024


def _params(sem, vmem_bytes):
    return pltpu.CompilerParams(dimension_semantics=sem,
                                vmem_limit_bytes=int(min(VMEM_CAP, max(vmem_bytes, 16 * 1024 * 1024))))


def _sigmoid(x):
    return 0.5 * jnp.tanh(0.5 * x) + 0.5


def _silu(x):
    h = 0.5 * x
    return h * jnp.tanh(h) + h


def _split3(x):
    hi = x.astype(BF16)
    r1 = x - hi.astype(F32)
    mid = r1.astype(BF16)
    return hi, mid, (r1 - mid.astype(F32)).astype(BF16)


def _tile(dim, cap, unit):
    if dim <= cap:
        return dim
    best = max(t for t in range(unit, cap + 1, unit) if dim % t == 0)
    return best


def _rmsnorm_body(x_ref, g_ref, o_ref):
    x = x_ref[...]
    ms = jnp.mean(x * x, axis=-1, keepdims=True)
    o_ref[...] = (x * lax.rsqrt(ms + EPS) * g_ref[...]).astype(o_ref.dtype)


def _rmsnorm(x, g, out_dtype, bm=256):
    m, d = x.shape
    bm = min(bm, m)
    return pl.pallas_call(
        _rmsnorm_body,
        grid=(m // bm,),
        in_specs=[pl.BlockSpec((bm, d), lambda i: (i, 0)),
                  pl.BlockSpec((1, d), lambda i: (0, 0))],
        out_specs=pl.BlockSpec((bm, d), lambda i: (i, 0)),
        out_shape=jax.ShapeDtypeStruct((m, d), out_dtype),
        compiler_params=_params(("parallel",), 6 * bm * d * 4),
        name="rmsnorm",
    )(x, g.reshape(1, d).astype(F32))


def _epilogue(acc, res_ref, o_ref, act):
    if act == "relu2":
        acc = jnp.square(jnp.maximum(acc, 0.0))
    if res_ref is not None:
        acc = res_ref[...] + acc
    o_ref[...] = acc.astype(o_ref.dtype)


def _proj_body(*refs, act, has_s, has_res, w_rows_are_outputs):
    if w_rows_are_outputs:
        mm = lambda a, w: lax.dot_general(a, w, (((1,), (1,)), ((), ())), preferred_element_type=F32)
    else:
        mm = lambda a, w: jnp.dot(a, w, preferred_element_type=F32)
    it = iter(refs)
    ap_ref = next(it)
    as_ref = next(it) if has_s else None
    w_ref = next(it)
    rp_ref = next(it) if has_res else None
    rs_ref = next(it) if has_res and has_s else None
    op_ref = next(it)
    os_ref = next(it) if has_s else None
    w16 = next(it)

    @pl.when(pl.program_id(1) == 0)
    def _():
        w16[...] = w_ref[...].astype(BF16)
        if has_s:
            _epilogue(mm(as_ref[...], w16[...]), rs_ref, os_ref, act)

    _epilogue(mm(ap_ref[...], w16[...]), rp_ref, op_ref, act)


def _proj(a_p, a_s, w, *, out_dtype, col0=0, ncols=None, act=None, res_p=None, res_s=None, w_rows_are_outputs=False):
    mp, kdim = a_p.shape
    ncols = w.shape[1 if w_rows_are_outputs else 2] - col0 if ncols is None else ncols
    has_s, has_res = a_s is not None, res_p is not None
    osz = jnp.dtype(out_dtype).itemsize
    ms = a_s.shape[0] if has_s else 0

    def footprint(bm, bn):
        return (2 * kdim * bn * 4 + kdim * bn * 2 + 2 * (bm + ms) * kdim * 2
                + (2 * osz + 2 * 4 * has_res + 4) * (bm + ms) * bn)

    for bm_cap, bn_cap in ((512, 1024), (1024, 512)):
        bm = _tile(mp, bm_cap, SUBLANES)
        bn = max(t for t in range(LANES, bn_cap + 1, LANES) if col0 % t == 0 and ncols % t == 0)
        if footprint(bm, bn) + PROJ_VMEM_SLACK <= VMEM_CAP:
            break
    c0 = col0 // bn
    vmem = footprint(bm, bn) + PROJ_VMEM_SLACK
    in_specs = [pl.BlockSpec((bm, kdim), lambda j, i: (i, 0))]
    args = [a_p]
    out_specs = [pl.BlockSpec((bm, bn), lambda j, i: (i, j))]
    out_shape = [jax.ShapeDtypeStruct((mp, ncols), out_dtype)]
    if has_s:
        in_specs.append(pl.BlockSpec((ms, kdim), lambda j, i: (0, 0)))
        args.append(a_s)
        out_specs.append(pl.BlockSpec((ms, bn), lambda j, i: (0, j)))
        out_shape.append(jax.ShapeDtypeStruct((ms, ncols), out_dtype))
    if w_rows_are_outputs:
        in_specs.append(pl.BlockSpec((None, bn, kdim), lambda j, i: (0, c0 + j, 0)))
    else:
        in_specs.append(pl.BlockSpec((None, kdim, bn), lambda j, i: (0, 0, c0 + j)))
    args.append(w)
    if has_res:
        in_specs.append(pl.BlockSpec((bm, bn), lambda j, i: (i, j)))
        args.append(res_p)
        if has_s:
            in_specs.append(pl.BlockSpec((ms, bn), lambda j, i: (0, j)))
            args.append(res_s)
    out = pl.pallas_call(
        functools.partial(_proj_body, act=act, has_s=has_s, has_res=has_res, w_rows_are_outputs=w_rows_are_outputs),
        grid=(ncols // bn, mp // bm),
        in_specs=in_specs,
        out_specs=out_specs,
        out_shape=out_shape,
        scratch_shapes=[pltpu.VMEM((bn, kdim) if w_rows_are_outputs else (kdim, bn), BF16)],
        compiler_params=_params(("parallel", "arbitrary"), vmem),
        name="proj",
    )(*args)
    return (out[0], out[1]) if has_s else (out[0], None)


def _kgrid_body(*refs, part_blocks, has_res):
    nparts = len(part_blocks)
    nk = sum(part_blocks)
    it = iter(refs)
    ap_refs = [next(it) for _ in range(nparts)]
    as_refs = [next(it) for _ in range(nparts)]
    w_ref = next(it)
    rp_ref = next(it) if has_res else None
    rs_ref = next(it) if has_res else None
    op_ref, os_ref, accp_ref, accs_ref = next(it), next(it), next(it), next(it)
    i, k = pl.program_id(1), pl.program_id(2)

    def reduce_k(a_refs, acc_ref, res_ref, o_ref):
        lo = 0
        for p, blocks in enumerate(part_blocks):
            hi = lo + blocks
            first, last = max(lo, 1), min(hi, nk - 1)
            dot = lambda p=p: jnp.dot(a_refs[p][...], w_ref[...], preferred_element_type=F32)
            if lo == 0:
                @pl.when(k == 0)
                def _(dot=dot):
                    acc_ref[...] = dot()
            if last > first:
                @pl.when((k >= first) & (k < last))
                def _(dot=dot):
                    acc_ref[...] += dot()
            if hi == nk:
                @pl.when(k == nk - 1)
                def _(dot=dot):
                    _epilogue(acc_ref[...] + dot(), res_ref, o_ref, None)
            lo = hi

    reduce_k(ap_refs, accp_ref, rp_ref, op_ref)

    @pl.when(i == 0)
    def _():
        reduce_k(as_refs, accs_ref, rs_ref, os_ref)


def _kgrid(parts_p, parts_s, w, *, res_p=None, res_s=None):
    mp, ms = parts_p[0].shape[0], parts_s[0].shape[0]
    kdim, n = w.shape
    assert sum(a.shape[1] for a in parts_p) == kdim
    bm = _tile(mp, 1024, SUBLANES)
    bn = _tile(n, 1024, LANES)
    bk = 2048
    while any(a.shape[1] % bk for a in parts_p):
        bk //= 2
    part_blocks = tuple(a.shape[1] // bk for a in parts_p)
    nk = sum(part_blocks)
    assert nk >= 2 and bk % LANES == 0
    has_res = res_p is not None
    vmem = (2 * len(parts_p) * (bm + ms) * bk * 2 + 2 * bk * bn * 2 + (5 + 2 * has_res) * (bm + ms) * bn * 4)
    lhs_specs = lambda rows, row_of: [
        pl.BlockSpec((rows, bk), lambda j, i, k, lo=lo, blocks=blocks: (row_of(i), jnp.clip(k - lo, 0, blocks - 1)))
        for lo, blocks in zip([sum(part_blocks[:p]) for p in range(len(part_blocks))], part_blocks)]
    in_specs = lhs_specs(bm, lambda i: i) + lhs_specs(ms, lambda i: 0)
    in_specs.append(pl.BlockSpec((bk, bn), lambda j, i, k: (k, j)))
    args = [*parts_p, *parts_s, w]
    if has_res:
        in_specs += [pl.BlockSpec((bm, bn), lambda j, i, k: (i, j)), pl.BlockSpec((ms, bn), lambda j, i, k: (0, j))]
        args += [res_p, res_s]
    return pl.pallas_call(
        functools.partial(_kgrid_body, part_blocks=part_blocks, has_res=has_res),
        grid=(n // bn, mp // bm, nk),
        in_specs=in_specs,
        out_specs=[pl.BlockSpec((bm, bn), lambda j, i, k: (i, j)), pl.BlockSpec((ms, bn), lambda j, i, k: (0, j))],
        out_shape=[jax.ShapeDtypeStruct((mp, n), F32), jax.ShapeDtypeStruct((ms, n), F32)],
        scratch_shapes=[pltpu.VMEM((bm, bn), F32), pltpu.VMEM((ms, bn), F32)],
        compiler_params=_params(("parallel", "arbitrary", "arbitrary"), vmem + 2 * 1024 * 1024),
        name="kgrid",
    )(*args)


CONV_HALO = 32
CONV_ROWS = 32
CONV_STRIP = 512


def _conv_prompt_body(a_ref, g_ref, w_ref, b_ref, lng_ref, lnb_ref, v_ref, hist_ref, ubuf, sbuf, cbuf,
                      *, tile, width, nt):
    t = pl.program_id(1)
    ch = ubuf.shape[1]

    @pl.when(t == 0)
    def _():
        ubuf[0:CONV_HALO, :] = jnp.zeros((CONV_HALO, ch), F32)

    ubuf[CONV_HALO:CONV_HALO + tile, :] = a_ref[...] * _sigmoid(g_ref[...])
    off0 = CONV_HALO - (width - 1)
    srows = CONV_HALO + tile - SUBLANES

    def strip(c, carry):
        col = pl.ds(pl.multiple_of(c * CONV_STRIP, CONV_STRIP), CONV_STRIP)
        for r in range(1, SUBLANES):
            sbuf[r, 0:srows, :] = ubuf[pl.ds(r, srows), col]
        for r0 in range(0, tile, CONV_ROWS):
            acc = jnp.broadcast_to(b_ref[:, col], (CONV_ROWS, CONV_STRIP))
            for k in range(width):
                whole, r = divmod(off0 + k, SUBLANES)
                if r == 0:
                    rows = ubuf[pl.ds(r0 + off0 + k, CONV_ROWS), col]
                else:
                    rows = sbuf[r, pl.ds(r0 + whole * SUBLANES, CONV_ROWS), :]
                acc = acc + w_ref[k:k + 1, col] * rows
            cbuf[r0:r0 + CONV_ROWS, col] = acc
        return carry

    lax.fori_loop(0, ch // CONV_STRIP, strip, 0)

    y = cbuf[...]
    yc = y - jnp.mean(y, axis=-1, keepdims=True)
    yn = yc * lax.rsqrt(jnp.mean(yc * yc, axis=-1, keepdims=True) + EPS)
    v_ref[...] = _silu(yn * lng_ref[...] + lnb_ref[...]).astype(v_ref.dtype)

    ubuf[0:CONV_HALO, :] = ubuf[tile:tile + CONV_HALO, :]

    @pl.when(t == nt - 1)
    def _():
        hist_ref[0] = ubuf[0:CONV_HALO, :]


def _conv_prompt(glu, conv_w, conv_b, ln_g, ln_b, batch, seqlen):
    width, ch = conv_w.shape
    tile = min(128, seqlen)
    nt = seqlen // tile
    assert width - 1 <= CONV_HALO <= tile and seqlen % tile == 0
    assert ch % CONV_STRIP == 0 and tile % CONV_ROWS == 0
    row = lambda b, t: (b * nt + t, 0)
    vec = pl.BlockSpec((1, ch), lambda b, t: (0, 0))
    return pl.pallas_call(
        functools.partial(_conv_prompt_body, tile=tile, width=width, nt=nt),
        grid=(batch, nt),
        in_specs=[pl.BlockSpec((tile, ch), row),
                  pl.BlockSpec((tile, ch), lambda b, t: (b * nt + t, 1)),
                  pl.BlockSpec((width, ch), lambda b, t: (0, 0)),
                  vec, vec, vec],
        out_specs=[pl.BlockSpec((tile, ch), row),
                   pl.BlockSpec((1, CONV_HALO, ch), lambda b, t: (b, 0, 0))],
        out_shape=[jax.ShapeDtypeStruct((batch * seqlen, ch), BF16),
                   jax.ShapeDtypeStruct((batch, CONV_HALO, ch), F32)],
        scratch_shapes=[pltpu.VMEM((CONV_HALO + tile, ch), F32),
                        pltpu.VMEM((SUBLANES, CONV_HALO + tile, CONV_STRIP), F32),
                        pltpu.VMEM((tile, ch), F32)],
        compiler_params=_params(("arbitrary", "arbitrary"), 32 * 1024 * 1024),
        name="conv_prompt",
    )(glu, glu, conv_w, conv_b.reshape(1, ch), ln_g.reshape(1, ch), ln_b.reshape(1, ch))


def _conv_sample_body(a_ref, g_ref, st_ref, w_ref, b_ref, lng_ref, lnb_ref, v_ref, nst_ref, *, width):
    u = a_ref[...] * _sigmoid(g_ref[...])
    y = b_ref[...] + w_ref[width - 1:width, :] * u
    for k in range(width - 1):
        y = y + w_ref[k:k + 1, :] * st_ref[k]
    yc = y - jnp.mean(y, axis=-1, keepdims=True)
    yn = yc * lax.rsqrt(jnp.mean(yc * yc, axis=-1, keepdims=True) + EPS)
    v_ref[...] = _silu(yn * lng_ref[...] + lnb_ref[...]).astype(v_ref.dtype)
    for k in range(width - 2):
        nst_ref[k] = st_ref[k + 1]
    nst_ref[width - 2] = u


def _conv_sample(glu, state, conv_w, conv_b, ln_g, ln_b):
    width, ch = conv_w.shape
    nb = glu.shape[0]
    bs = _tile(nb, 2 * SUBLANES, SUBLANES)
    assert nb % bs == 0
    vec = pl.BlockSpec((1, ch), lambda i: (0, 0))
    return pl.pallas_call(
        functools.partial(_conv_sample_body, width=width),
        grid=(nb // bs,),
        in_specs=[pl.BlockSpec((bs, ch), lambda i: (i, 0)),
                  pl.BlockSpec((bs, ch), lambda i: (i, 1)),
                  pl.BlockSpec((width - 1, bs, ch), lambda i: (0, i, 0)),
                  pl.BlockSpec((width, ch), lambda i: (0, 0)),
                  vec, vec, vec],
        out_specs=[pl.BlockSpec((bs, ch), lambda i: (i, 0)),
                   pl.BlockSpec((width - 1, bs, ch), lambda i: (0, i, 0))],
        out_shape=[jax.ShapeDtypeStruct((nb, ch), BF16),
                   jax.ShapeDtypeStruct(state.shape, F32)],
        compiler_params=_params(("parallel",), 32 * 1024 * 1024),
        name="conv_sample",
    )(glu, glu, state, conv_w, conv_b.reshape(1, ch), ln_g.reshape(1, ch), ln_b.reshape(1, ch))


def _expand_heads(v, heads, head_dim):
    rows = v.shape[0]
    per_tile = LANES // head_dim
    lane = lax.broadcasted_iota(jnp.int32, (rows, LANES), 1)
    tiles = []
    for j in range(heads // per_tile):
        h0 = j * per_tile
        t = jnp.broadcast_to(v[:, h0:h0 + 1], (rows, LANES))
        for i in range(1, per_tile):
            t = jnp.where(lane >= i * head_dim, jnp.broadcast_to(v[:, h0 + i:h0 + i + 1], (rows, LANES)), t)
        tiles.append(t)
    return jnp.concatenate(tiles, axis=1)


def _pad_heads(v, groups):
    hg = v.shape[0] // groups
    return jnp.pad(v.reshape(groups, hg).astype(F32), ((0, 0), (0, LANES - hg))).reshape(1, groups * LANES)


def _group_rmsnorm_gate(y, xs, z, dskip, normg):
    y = (y + dskip * xs) * _silu(z)
    return y * lax.rsqrt(jnp.mean(y * y, axis=-1, keepdims=True) + EPS) * normg


def _ssd_prompt_body(xs_ref, z_ref, b_ref, c_ref, dt_ref, wx_ref, wb_ref, wc_ref, bx_ref, bb_ref, bc_ref,
                     dtb_ref, alog_ref, dskip_ref, normg_ref, bd_ref, rep_ref,
                     y_ref, st_ref, state, xbuf, bbuf, cbuf, *, q, nc, head_dim, cw):
    c = pl.program_id(2)
    gw = xbuf.shape[1]
    n = bbuf.shape[1]
    hg = gw // head_dim
    halo = SUBLANES

    @pl.when(c == 0)
    def _():
        state[...] = jnp.zeros(state.shape, F32)
        xbuf[0:halo, :] = jnp.zeros((halo, gw), F32)
        bbuf[0:halo, :] = jnp.zeros((halo, n), F32)
        cbuf[0:halo, :] = jnp.zeros((halo, n), F32)

    def causal_conv(raw_ref, buf, w_ref, bias_ref):
        buf[halo:halo + q, :] = raw_ref[...]
        rows = buf[...]
        acc = bias_ref[...] + w_ref[cw - 1:cw, :] * rows[halo:]
        for j in range(1, cw):
            acc = acc + w_ref[cw - 1 - j:cw - j, :] * pltpu.roll(rows, j, axis=0)[halo:]
        buf[0:halo, :] = rows[q:q + halo]
        return _silu(acc)

    xs = causal_conv(xs_ref, xbuf, wx_ref, bx_ref)
    bm = causal_conv(b_ref, bbuf, wb_ref, bb_ref)
    cm = causal_conv(c_ref, cbuf, wc_ref, bc_ref)

    dt = jax.nn.softplus(dt_ref[...] + dtb_ref[...])
    dta = dt * (-jnp.exp(alog_ref[...]))
    qi = lax.broadcasted_iota(jnp.int32, (q, q), 0)
    si = lax.broadcasted_iota(jnp.int32, (q, q), 1)
    causal = qi >= si
    tri = causal.astype(BF16)
    hi, mid, lo = _split3(dta)
    cs = (jnp.dot(tri, hi, preferred_element_type=F32) + jnp.dot(tri, mid, preferred_element_type=F32)
          + jnp.dot(tri, lo, preferred_element_type=F32))
    cs_t = cs.T
    dt_t = dt.T
    per_head = jnp.concatenate([jnp.exp(cs), dt * jnp.exp(cs[q - 1:q, :] - cs)], axis=0)
    spread = rep_ref[...]
    per_lane = sum(jnp.dot(piece, spread, preferred_element_type=F32) for piece in _split3(per_head))
    exp_cs_x = per_lane[0:q]
    xdt_end = xs * per_lane[q:2 * q]

    xs16 = xs.astype(BF16)
    cm16 = cm.astype(BF16)
    cb = lax.dot_general(cm16, bm.astype(BF16), (((1,), (1,)), ((), ())), preferred_element_type=F32)

    hb = MXU_DIM // head_dim
    slab = hb * head_dim
    cb = jnp.where(causal, cb, 0.0)
    blockdiag = bd_ref[...]
    y_parts = []
    for s in range(hg // hb):
        lhs = []
        for j in range(hb):
            h = s * hb + j
            seg = cs[:, h:h + 1] - cs_t[h:h + 1, :]
            lmat = jnp.exp(jnp.where(causal, seg, 0.0))
            lhs.append((cb * lmat * dt_t[h:h + 1, :]).astype(BF16))
        xslab = xs16[:, s * slab:(s + 1) * slab]
        rhs = jnp.concatenate([xslab] * hb, axis=0) * blockdiag
        y_parts.append(jnp.dot(jnp.concatenate(lhs, axis=1), rhs, preferred_element_type=F32))
    y = jnp.concatenate(y_parts, axis=1)

    st = state[...]
    y = y + jnp.dot(cm16, st.astype(BF16), preferred_element_type=F32) * exp_cs_x
    upd = jnp.dot(bm.T.astype(BF16), xdt_end.astype(BF16), preferred_element_type=F32)
    new_state = st * exp_cs_x[q - 1:q, :] + upd
    state[...] = new_state

    y_ref[...] = _group_rmsnorm_gate(y, xs, z_ref[...], dskip_ref[...], normg_ref[...]).astype(y_ref.dtype)

    @pl.when(c == nc - 1)
    def _():
        st_ref[0] = new_state.T


N_SSD_INPUTS = 17


def _ssd_prompt_and_casts_body(*refs, n_cast, **kw):
    ins, riders = refs[:N_SSD_INPUTS], refs[N_SSD_INPUTS:N_SSD_INPUTS + n_cast]
    outs = refs[N_SSD_INPUTS + n_cast:N_SSD_INPUTS + n_cast + 2]
    rider_outs = refs[N_SSD_INPUTS + n_cast + 2:N_SSD_INPUTS + 2 * n_cast + 2]
    scratch = refs[N_SSD_INPUTS + 2 * n_cast + 2:]
    for src, dst in zip(riders, rider_outs):
        dst[...] = src[...].astype(dst.dtype)
    _ssd_prompt_body(*ins, *outs, *scratch, **kw)


def _ssd_prompt(xs, z, bc, dt, wx, wbc, bx, bbc, dtb, alog, dskip, normg, batch, seqlen, groups, head_dim, n,
                cast_riders=()):
    m, d_ssm = xs.shape
    gw = d_ssm // groups
    hg = gw // head_dim
    cw = wx.shape[0]
    q = min(128, seqlen)
    nc = seqlen // q
    hb = MXU_DIM // head_dim
    assert seqlen % q == 0 and q % LANES == 0 and n == LANES and hg <= LANES
    assert gw % MXU_DIM == 0 and MXU_DIM % head_dim == 0 and cw - 1 <= SUBLANES
    row = lambda b, g, c: (b * nc + c, g)
    gcol = lambda b, g, c: (0, g)
    in_specs = [
        pl.BlockSpec((q, gw), row),
        pl.BlockSpec((q, gw), row),
        pl.BlockSpec((q, n), row),
        pl.BlockSpec((q, n), lambda b, g, c: (b * nc + c, groups + g)),
        pl.BlockSpec((q, LANES), row),
        pl.BlockSpec((cw, gw), gcol),
        pl.BlockSpec((cw, n), gcol),
        pl.BlockSpec((cw, n), lambda b, g, c: (0, groups + g)),
        pl.BlockSpec((1, gw), gcol),
        pl.BlockSpec((1, n), gcol),
        pl.BlockSpec((1, n), lambda b, g, c: (0, groups + g)),
        pl.BlockSpec((1, LANES), gcol),
        pl.BlockSpec((1, LANES), gcol),
        pl.BlockSpec((1, gw), gcol),
        pl.BlockSpec((1, gw), gcol),
        pl.BlockSpec((hb * q, MXU_DIM), lambda b, g, c: (0, 0)),
        pl.BlockSpec((LANES, gw), lambda b, g, c: (0, 0)),
    ]
    spread = (jnp.arange(LANES)[:, None] == (jnp.arange(gw)[None, :] // head_dim)).astype(BF16)
    blockdiag = ((jnp.arange(hb * q)[:, None] // q) == (jnp.arange(MXU_DIM)[None, :] // head_dim)).astype(BF16)
    assert len(in_specs) == N_SSD_INPUTS
    steps = batch * groups * nc
    rider_specs = []
    for r in cast_riders:
        slab = r.shape[0] // steps
        assert r.shape[0] % steps == 0 and slab % (2 * SUBLANES) == 0
        rider_specs.append(pl.BlockSpec((slab, r.shape[1]), lambda b, g, c: ((b * groups + g) * nc + c, 0)))
    return pl.pallas_call(
        functools.partial(_ssd_prompt_and_casts_body, n_cast=len(cast_riders), q=q, nc=nc, head_dim=head_dim, cw=cw),
        grid=(batch, groups, nc),
        in_specs=in_specs + rider_specs,
        out_specs=[pl.BlockSpec((q, gw), row),
                   pl.BlockSpec((1, gw, n), lambda b, g, c: (b, g, 0))] + rider_specs,
        out_shape=[jax.ShapeDtypeStruct((m, d_ssm), BF16),
                   jax.ShapeDtypeStruct((batch, d_ssm, n), F32)]
                  + [jax.ShapeDtypeStruct(r.shape, BF16) for r in cast_riders],
        scratch_shapes=[pltpu.VMEM((n, gw), F32),
                        pltpu.VMEM((SUBLANES + q, gw), F32),
                        pltpu.VMEM((SUBLANES + q, n), F32),
                        pltpu.VMEM((SUBLANES + q, n), F32)],
        compiler_params=_params(("arbitrary", "arbitrary", "arbitrary"), 32 * 1024 * 1024),
        name="ssd_prompt",
    )(xs, z, bc, bc, dt, wx, wbc, wbc, bx, bbc, bbc, dtb, alog, dskip, normg, blockdiag, spread, *cast_riders)


def _ssm_sample_pre_body(xs_ref, b_ref, c_ref, dt_ref, hx_ref, hb_ref, hc_ref, wx_ref, wb_ref, wc_ref,
                         bx_ref, bb_ref, bc_ref, dtb_ref, alog_ref,
                         xs_o, xdt_o, b_o, c_o, decay_o, *, cw, head_dim):
    def conv(new_ref, hist_ref, w_ref, bias_ref):
        acc = bias_ref[...] + w_ref[cw - 1:cw, :] * new_ref[...]
        for k in range(cw - 1):
            acc = acc + w_ref[k:k + 1, :] * hist_ref[k]
        return _silu(acc)

    xs = conv(xs_ref, hx_ref, wx_ref, bx_ref)
    b_o[...] = conv(b_ref, hb_ref, wb_ref, bb_ref)
    c_o[...] = conv(c_ref, hc_ref, wc_ref, bc_ref)
    dt = jax.nn.softplus(dt_ref[...] + dtb_ref[...])
    decay_o[...] = jnp.exp(dt * (-jnp.exp(alog_ref[...])))
    xs_o[...] = xs
    xdt_o[...] = (xs * _expand_heads(dt, xs.shape[1] // head_dim, head_dim)).astype(xdt_o.dtype)


def _ssm_sample_pre(xs, bc, dt, hist, wx, wbc, bx, bbc, dtb, alog, groups, head_dim, n):
    nb, d_ssm = xs.shape
    gw = d_ssm // groups
    hg = gw // head_dim
    cw = wx.shape[0]
    xoff = d_ssm // n
    gcol = lambda g: (0, g)
    in_specs = [
        pl.BlockSpec((nb, gw), gcol),
        pl.BlockSpec((nb, n), gcol),
        pl.BlockSpec((nb, n), lambda g: (0, groups + g)),
        pl.BlockSpec((nb, LANES), gcol),
        pl.BlockSpec((cw - 1, nb, gw), lambda g: (0, 0, g)),
        pl.BlockSpec((cw - 1, nb, n), lambda g: (0, 0, xoff + g)),
        pl.BlockSpec((cw - 1, nb, n), lambda g: (0, 0, xoff + groups + g)),
        pl.BlockSpec((cw, gw), gcol),
        pl.BlockSpec((cw, n), gcol),
        pl.BlockSpec((cw, n), lambda g: (0, groups + g)),
        pl.BlockSpec((1, gw), gcol),
        pl.BlockSpec((1, n), gcol),
        pl.BlockSpec((1, n), lambda g: (0, groups + g)),
        pl.BlockSpec((1, LANES), gcol),
        pl.BlockSpec((1, LANES), gcol),
    ]
    return pl.pallas_call(
        functools.partial(_ssm_sample_pre_body, cw=cw, head_dim=head_dim),
        grid=(groups,),
        in_specs=in_specs,
        out_specs=[pl.BlockSpec((nb, gw), gcol),
                   pl.BlockSpec((nb, gw), gcol),
                   pl.BlockSpec((nb, n), gcol),
                   pl.BlockSpec((nb, n), gcol),
                   pl.BlockSpec((nb, LANES), gcol)],
        out_shape=[jax.ShapeDtypeStruct((nb, d_ssm), F32),
                   jax.ShapeDtypeStruct((nb, d_ssm), BF16),
                   jax.ShapeDtypeStruct((nb, groups * n), F32),
                   jax.ShapeDtypeStruct((nb, groups * n), F32),
                   jax.ShapeDtypeStruct((nb, groups * LANES), F32)],
        compiler_params=_params(("parallel",), 32 * 1024 * 1024),
        name="ssm_sample_pre",
    )(xs, bc, bc, dt, hist, hist, hist, wx, wbc, wbc, bx, bbc, bbc, dtb, alog)


def _ssm_sample_state_body(decay_ref, xdt_t_ref, bg_ref, cg_ref, xs_ref, z_ref, dskip_ref, normg_ref, st_ref,
                           y_ref, nst_ref, *, groups, hg, head_dim):
    b = pl.program_id(0)
    nb = bg_ref.shape[1]
    n = bg_ref.shape[2]
    gw = hg * head_dim
    heads = groups * hg
    is_b = lax.broadcasted_iota(jnp.int32, (nb, n), 0) == b
    for g in range(groups):
        lo = g * gw
        bsel = jnp.where(is_b, bg_ref[g], 0.0).astype(BF16)
        outer = jnp.dot(xdt_t_ref[lo:lo + gw, :], bsel, preferred_element_type=F32)
        for h in range(hg):
            r0 = lo + h * head_dim
            d = decay_ref[b * heads + g * hg + h]
            nst_ref[0, r0:r0 + head_dim, :] = (st_ref[0, r0:r0 + head_dim, :] * d
                                               + outer[h * head_dim:(h + 1) * head_dim, :])
        c_row = jnp.broadcast_to(cg_ref[g, pl.ds(b, 1), :], (SUBLANES, n)).astype(BF16)
        y = lax.dot_general(c_row, nst_ref[0, lo:lo + gw, :].astype(BF16), (((1,), (1,)), ((), ())),
                            preferred_element_type=F32)[0:1, :]
        y = _group_rmsnorm_gate(y, xs_ref[pl.ds(b, 1), lo:lo + gw], z_ref[pl.ds(b, 1), lo:lo + gw],
                                dskip_ref[:, lo:lo + gw], normg_ref[:, lo:lo + gw])
        y_ref[0, :, lo:lo + gw] = y.astype(y_ref.dtype)


def _ssm_sample_state(decay, xdt_t, bg, cg, xs, z, dskip, normg, state, groups, head_dim):
    nb, d_ssm = xs.shape
    n = state.shape[-1]
    hg = d_ssm // groups // head_dim
    full2 = lambda b: (0, 0)
    full3 = lambda b: (0, 0, 0)
    return pl.pallas_call(
        functools.partial(_ssm_sample_state_body, groups=groups, hg=hg, head_dim=head_dim),
        grid=(nb,),
        in_specs=[pl.BlockSpec(memory_space=pltpu.SMEM),
                  pl.BlockSpec((d_ssm, nb), full2),
                  pl.BlockSpec((groups, nb, n), full3),
                  pl.BlockSpec((groups, nb, n), full3),
                  pl.BlockSpec((nb, d_ssm), full2),
                  pl.BlockSpec((nb, d_ssm), full2),
                  pl.BlockSpec((1, d_ssm), full2),
                  pl.BlockSpec((1, d_ssm), full2),
                  pl.BlockSpec((1, d_ssm, n), lambda b: (b, 0, 0))],
        out_specs=[pl.BlockSpec((1, 1, d_ssm), lambda b: (b, 0, 0)),
                   pl.BlockSpec((1, d_ssm, n), lambda b: (b, 0, 0))],
        out_shape=[jax.ShapeDtypeStruct((nb, 1, d_ssm), F32),
                   jax.ShapeDtypeStruct((nb, d_ssm, n), F32)],
        compiler_params=_params(("arbitrary",), 48 * 1024 * 1024),
        name="ssm_sample_state",
    )(decay, xdt_t, bg, cg, xs, z, dskip, normg, state)


def _softmax_rows(s):
    e = jnp.exp(s - jnp.max(s, axis=-1, keepdims=True))
    return e / jnp.sum(e, axis=-1, keepdims=True)


def _attn_prompt_body(q_ref, k_ref, v_ref, o_ref, *, scale):
    s = lax.dot_general(q_ref[...], k_ref[...].astype(BF16), (((1,), (1,)), ((), ())),
                        preferred_element_type=F32) * scale
    p = _softmax_rows(s)
    o_ref[...] = jnp.dot(p.astype(BF16), v_ref[...].astype(BF16), preferred_element_type=F32).astype(o_ref.dtype)


def _attn_prompt(q, k, v, batch, seqlen, n_mem, heads):
    m, d = q.shape
    hd = d // heads
    tq = min(512, seqlen)
    nq = seqlen // tq
    return pl.pallas_call(
        functools.partial(_attn_prompt_body, scale=hd ** -0.5),
        grid=(batch, heads, nq),
        in_specs=[pl.BlockSpec((tq, hd), lambda b, h, i: (b * nq + i, h)),
                  pl.BlockSpec((n_mem, hd), lambda b, h, i: (b, h)),
                  pl.BlockSpec((n_mem, hd), lambda b, h, i: (b, h))],
        out_specs=pl.BlockSpec((tq, hd), lambda b, h, i: (b * nq + i, h)),
        out_shape=jax.ShapeDtypeStruct((m, d), BF16),
        compiler_params=_params(("parallel", "parallel", "parallel"), 32 * 1024 * 1024),
        name="attn_prompt",
    )(q, k, v)


def _attn_sample_body(q_ref, k_ref, v_ref, o_ref, *, scale, chunk):
    q = q_ref[0]
    n_mem = k_ref.shape[2]
    s = jnp.concatenate(
        [jnp.sum(k_ref[0, 0, m0:m0 + chunk] * q[None], axis=-1, keepdims=True) for m0 in range(0, n_mem, chunk)],
        axis=0) * scale
    e = jnp.exp(s - jnp.max(s, axis=0, keepdims=True))
    p = e / jnp.sum(e, axis=0, keepdims=True)
    o = jnp.sum(p[0:chunk] * v_ref[0, 0, 0:chunk], axis=0)
    for m0 in range(chunk, n_mem, chunk):
        o = o + jnp.sum(p[m0:m0 + chunk] * v_ref[0, 0, m0:m0 + chunk], axis=0)
    o_ref[0] = o


def _attn_sample(q, k, v):
    _, nb, n_mem, heads, hd = k.shape
    chunk = min(32, n_mem)
    assert n_mem % chunk == 0
    kv_spec = pl.BlockSpec((1, 1, n_mem, heads, hd), lambda b: (0, b, 0, 0, 0))
    return pl.pallas_call(
        functools.partial(_attn_sample_body, scale=hd ** -0.5, chunk=chunk),
        grid=(nb,),
        in_specs=[pl.BlockSpec((1, heads, hd), lambda b: (b, 0, 0)), kv_spec, kv_spec],
        out_specs=pl.BlockSpec((1, heads, hd), lambda b: (b, 0, 0)),
        out_shape=jax.ShapeDtypeStruct((nb, heads, hd), F32),
        compiler_params=_params(("parallel",), 4 * n_mem * SUBLANES * hd * 4 + 8 * 1024 * 1024),
        name="attn_sample",
    )(q, k, v)


def kernel(x_prompt, x_sample, mem_prompt, state_conv, state_ssm_conv, state_ssm, cache_mem_k, cache_mem_v,
           g_mix, w_in, conv_w, conv_b, ln_g, ln_b, ssm_conv_w, ssm_conv_b, dt_bias, a_log, d_skip,
           ssm_norm_g, w_out, g_xattn, g_mem, w_q, w_k, w_v, w_o, g_mlp, w_up, w_down, g_final):
    depth = g_mix.shape[0]
    assert depth == 1
    batch, seqlen, d = x_prompt.shape
    nb = x_sample.shape[0]
    assert x_sample.shape[1] == 1
    d_conv = conv_w.shape[-1]
    c_xbc = ssm_conv_w.shape[-1]
    d_ssm = ssm_norm_g.shape[-1]
    heads = dt_bias.shape[-1]
    head_dim = d_ssm // heads
    n = state_ssm.shape[-1]
    groups = (c_xbc - d_ssm) // (2 * n)
    hg = heads // groups
    n_mem, xa_heads = cache_mem_k.shape[2], cache_mem_k.shape[3]
    cw = ssm_conv_w.shape[1]
    assert n == LANES and hg <= LANES

    o_z, o_x = 2 * d_conv, 2 * d_conv + d_ssm
    o_b, o_dt = o_x + d_ssm, o_x + c_xbc
    w_in_t = jnp.swapaxes(w_in, 1, 2)
    w_dt_t = jnp.pad(w_in_t[0, o_dt:].reshape(groups, hg, d), ((0, 0), (0, LANES - hg), (0, 0)))
    w_dt_t = w_dt_t.reshape(1, groups * LANES, d)
    wx, wbc = ssm_conv_w[0][:, :d_ssm], ssm_conv_w[0][:, d_ssm:]
    bx, bbc = ssm_conv_b[0][:d_ssm].reshape(1, d_ssm), ssm_conv_b[0][d_ssm:].reshape(1, c_xbc - d_ssm)
    dtb, alog = _pad_heads(dt_bias[0], groups), _pad_heads(a_log[0], groups)
    dskip = jnp.repeat(d_skip[0].astype(F32), head_dim).reshape(1, d_ssm)
    normg = ssm_norm_g[0].reshape(1, d_ssm)

    mp = batch * seqlen
    xp, xsm = x_prompt.reshape(mp, d), x_sample.reshape(nb, d)
    norm2 = lambda a_p, a_s, g, dtype: (_rmsnorm(a_p, g, dtype), _rmsnorm(a_s, g, dtype))
    in_proj = functools.partial(_proj, *norm2(xp, xsm, g_mix[0], BF16), out_dtype=F32, w_rows_are_outputs=True)
    glu, glu_s = in_proj(w_in_t, col0=0, ncols=o_z)
    z, z_s = in_proj(w_in_t, col0=o_z, ncols=d_ssm)
    xs, xs_s = in_proj(w_in_t, col0=o_x, ncols=d_ssm)
    bc, bc_s = in_proj(w_in_t, col0=o_b, ncols=o_dt - o_b)
    dt, dt_s = in_proj(w_dt_t)

    v_p, conv_hist = _conv_prompt(glu, conv_w[0], conv_b[0], ln_g[0], ln_b[0], batch, seqlen)
    y_p, st_p, w_out16, w_down16 = _ssd_prompt(xs, z, bc, dt, wx, wbc, bx, bbc, dtb, alog, dskip, normg,
                                               batch, seqlen, groups, head_dim, n,
                                               cast_riders=(w_out[0], w_down[0]))
    keep = conv_w.shape[1] - 1
    new_conv_p = conv_hist[:, CONV_HALO - keep:, :]
    tail = lambda a: a.reshape(batch, seqlen, -1)[:, seqlen - (cw - 1):, :]
    new_ssm_conv_p = jnp.concatenate([tail(xs), tail(bc)], axis=-1)

    v_s, new_conv_t = _conv_sample(glu_s, jnp.swapaxes(state_conv[0], 0, 1), conv_w[0], conv_b[0], ln_g[0], ln_b[0])
    new_conv_s = jnp.swapaxes(new_conv_t, 0, 1)[None]
    hist = jnp.swapaxes(state_ssm_conv[0], 0, 1)
    xs_c, xdt, b_c, c_c, decay = _ssm_sample_pre(xs_s, bc_s, dt_s, hist, wx, wbc, bx, bbc, dtb, alog,
                                                  groups, head_dim, n)
    decay_flat = decay.reshape(nb, groups, LANES)[:, :, :hg].reshape(nb * heads)
    to_groups = lambda a: jnp.swapaxes(a.reshape(nb, groups, n), 0, 1)
    y_s, st_s = _ssm_sample_state(decay_flat, xdt.T, to_groups(b_c), to_groups(c_c), xs_c, z_s, dskip, normg,
                                  state_ssm[0].reshape(nb, d_ssm, n), groups, head_dim)
    new_ssm_conv_s = jnp.swapaxes(
        jnp.concatenate([hist[1:], jnp.concatenate([xs_s, bc_s], axis=-1)[None]], axis=0), 0, 1)

    x1, x1_s = _kgrid([v_p, y_p], [v_s, y_s.reshape(nb, d_ssm).astype(BF16)], w_out16,
                      res_p=xp, res_s=xsm)
    q, q_s = _proj(*norm2(x1, x1_s, g_xattn[0], BF16), w_q, out_dtype=BF16)
    mem_n = _rmsnorm(mem_prompt.reshape(batch * n_mem, d), g_mem[0], BF16)
    k_p, _ = _proj(mem_n, None, w_k, out_dtype=F32)
    v_mem_p, _ = _proj(mem_n, None, w_v, out_dtype=F32)
    o = _attn_prompt(q, k_p, v_mem_p, batch, seqlen, n_mem, xa_heads)
    o_s = _attn_sample(q_s.astype(F32).reshape(nb, xa_heads, d // xa_heads), cache_mem_k,
                       cache_mem_v).reshape(nb, d).astype(BF16)
    x2, x2_s = _proj(o, o_s, w_o, out_dtype=F32, res_p=x1, res_s=x1_s)
    up, up_s = _proj(*norm2(x2, x2_s, g_mlp[0], BF16), w_up, out_dtype=BF16, act="relu2")
    x3, x3_s = _kgrid([up], [up_s], w_down16, res_p=x2, res_s=x2_s)
    y_prompt, y_sample = norm2(x3, x3_s, g_final, F32)

    kv_shape = (1, batch, n_mem, xa_heads, d // xa_heads)
    return (y_prompt.reshape(batch, seqlen, d), y_sample.reshape(nb, 1, d),
            new_conv_p[None], new_ssm_conv_p[None], st_p.reshape(1, batch, heads, head_dim, n),
            k_p.reshape(kv_shape), v_mem_p.reshape(kv_shape),
            new_conv_s, new_ssm_conv_s[None], st_s.reshape(1, nb, heads, head_dim, n))
```

```python
import functools

import jax
import jax.numpy as jnp
from jax import lax
from jax.experimental import pallas as pl
from jax.experimental.pallas import tpu as pltpu

F32 = jnp.float32
BF16 = jnp.bfloat16
EPS = 1e-5
LANES = 128
SUBLANES = 8
MXU_DIM = 256
VMEM_CAP = 62 * 1024 * 1024
PROJ_VMEM_SLACK = 4 * 1024 * 1024


def _params(sem, vmem_bytes):
    return pltpu.CompilerParams(dimension_semantics=sem,
                                vmem_limit_bytes=int(min(VMEM_CAP, max(vmem_bytes, 16 * 1024 * 1024))))


def _sigmoid(x):
    return 0.5 * jnp.tanh(0.5 * x) + 0.5


def _silu(x):
    h = 0.5 * x
    return h * jnp.tanh(h) + h


def _split3(x):
    hi = x.astype(BF16)
    r1 = x - hi.astype(F32)
    mid = r1.astype(BF16)
    return hi, mid, (r1 - mid.astype(F32)).astype(BF16)


def _tile(dim, cap, unit):
    if dim <= cap:
        return dim
    best = max(t for t in range(unit, cap + 1, unit) if dim % t == 0)
    return best


def _rmsnorm_body(x_ref, g_ref, o_ref):
    x = x_ref[...]
    ms = jnp.mean(x * x, axis=-1, keepdims=True)
    o_ref[...] = (x * lax.rsqrt(ms + EPS) * g_ref[...]).astype(o_ref.dtype)


def _rmsnorm(x, g, out_dtype, bm=256):
    m, d = x.shape
    bm = min(bm, m)
    return pl.pallas_call(
        _rmsnorm_body,
        grid=(m // bm,),
        in_specs=[pl.BlockSpec((bm, d), lambda i: (i, 0)),
                  pl.BlockSpec((1, d), lambda i: (0, 0))],
        out_specs=pl.BlockSpec((bm, d), lambda i: (i, 0)),
        out_shape=jax.ShapeDtypeStruct((m, d), out_dtype),
        compiler_params=_params(("parallel",), 6 * bm * d * 4),
        name="rmsnorm",
    )(x, g.reshape(1, d).astype(F32))


def _epilogue(acc, res_ref, o_ref, act):
    if act == "relu2":
        acc = jnp.square(jnp.maximum(acc, 0.0))
    if res_ref is not None:
        acc = res_ref[...] + acc
    o_ref[...] = acc.astype(o_ref.dtype)


def _proj_body(*refs, act, has_s, has_res, w_rows_are_outputs, n_panels):
    if w_rows_are_outputs:
        mm = lambda a, w: lax.dot_general(a, w, (((1,), (1,)), ((), ())), preferred_element_type=F32)
    else:
        mm = lambda a, w: jnp.dot(a, w, preferred_element_type=F32)
    it = iter(refs)
    ap_ref = next(it)
    as_ref = next(it) if has_s else None
    w_ref = next(it)
    rp_ref = next(it) if has_res else None
    rs_ref = next(it) if has_res and has_s else None
    op_ref = next(it)
    os_ref = next(it) if has_s else None
    w16 = next(it)
    p, i = pl.program_id(0), pl.program_id(1)
    chunk = w_ref.shape[0]

    @pl.when(p < n_panels)
    def _():
        w16[p % 2, pl.ds(pl.multiple_of(i * chunk, chunk), chunk), :] = w_ref[...].astype(BF16)

    @pl.when(p >= 1)
    def _():
        panel = w16.at[(p - 1) % 2]
        if has_s:
            @pl.when(i == 0)
            def _():
                _epilogue(mm(as_ref[...], panel[...]), rs_ref, os_ref, act)
        _epilogue(mm(ap_ref[...], panel[...]), rp_ref, op_ref, act)


def _proj(a_p, a_s, w, *, out_dtype, col0=0, ncols=None, act=None, res_p=None, res_s=None, w_rows_are_outputs=False):
    mp, kdim = a_p.shape
    ncols = w.shape[1 if w_rows_are_outputs else 2] - col0 if ncols is None else ncols
    has_s, has_res = a_s is not None, res_p is not None
    osz = jnp.dtype(out_dtype).itemsize
    ms = a_s.shape[0] if has_s else 0

    def footprint(bm, bn):
        return (2 * kdim * bn * 2 + 2 * kdim * bn * 4 // (mp // bm) + 2 * (bm + ms) * kdim * 2
                + (2 * osz + 2 * 4 * has_res + 4) * (bm + ms) * bn)

    bm = _tile(mp, 1024, SUBLANES)
    ni = mp // bm
    for bn_cap in (1024, 512, 256):
        bn = max(t for t in range(LANES, bn_cap + 1, LANES) if col0 % t == 0 and ncols % t == 0)
        if footprint(bm, bn) + PROJ_VMEM_SLACK <= VMEM_CAP:
            break
    c0, nj = col0 // bn, ncols // bn
    vmem = footprint(bm, bn) + PROJ_VMEM_SLACK
    row = lambda p, i: jnp.where(p == 0, 0, i)
    col = lambda p: jnp.maximum(p - 1, 0)
    nxt = lambda p: jnp.minimum(p, nj - 1)
    in_specs = [pl.BlockSpec((bm, kdim), lambda p, i: (row(p, i), 0))]
    args = [a_p]
    out_specs = [pl.BlockSpec((bm, bn), lambda p, i: (row(p, i), col(p)))]
    out_shape = [jax.ShapeDtypeStruct((mp, ncols), out_dtype)]
    if has_s:
        in_specs.append(pl.BlockSpec((ms, kdim), lambda p, i: (0, 0)))
        args.append(a_s)
        out_specs.append(pl.BlockSpec((ms, bn), lambda p, i: (0, col(p))))
        out_shape.append(jax.ShapeDtypeStruct((ms, ncols), out_dtype))
    if w_rows_are_outputs:
        chunk = bn // ni
        in_specs.append(pl.BlockSpec((None, chunk, kdim), lambda p, i: (0, (c0 + nxt(p)) * ni + i, 0)))
    else:
        chunk = kdim // ni
        in_specs.append(pl.BlockSpec((None, chunk, bn), lambda p, i: (0, i, c0 + nxt(p))))
    assert chunk * ni == (bn if w_rows_are_outputs else kdim) and chunk % (2 * SUBLANES) == 0
    args.append(w)
    if has_res:
        in_specs.append(pl.BlockSpec((bm, bn), lambda p, i: (row(p, i), col(p))))
        args.append(res_p)
        if has_s:
            in_specs.append(pl.BlockSpec((ms, bn), lambda p, i: (0, col(p))))
            args.append(res_s)
    out = pl.pallas_call(
        functools.partial(_proj_body, act=act, has_s=has_s, has_res=has_res, w_rows_are_outputs=w_rows_are_outputs,
                          n_panels=nj),
        grid=(nj + 1, ni),
        in_specs=in_specs,
        out_specs=out_specs,
        out_shape=out_shape,
        scratch_shapes=[pltpu.VMEM((2, bn, kdim) if w_rows_are_outputs else (2, kdim, bn), BF16)],
        compiler_params=_params(("arbitrary", "arbitrary"), vmem),
        name="proj",
    )(*args)
    return (out[0], out[1]) if has_s else (out[0], None)


def _kgrid_body(*refs, part_blocks, has_res):
    nparts = len(part_blocks)
    nk = sum(part_blocks)
    it = iter(refs)
    ap_refs = [next(it) for _ in range(nparts)]
    as_refs = [next(it) for _ in range(nparts)]
    w_ref = next(it)
    rp_ref = next(it) if has_res else None
    rs_ref = next(it) if has_res else None
    op_ref, os_ref, accp_ref, accs_ref = next(it), next(it), next(it), next(it)
    i, k = pl.program_id(1), pl.program_id(2)

    def reduce_k(a_refs, acc_ref, res_ref, o_ref):
        lo = 0
        for p, blocks in enumerate(part_blocks):
            hi = lo + blocks
            first, last = max(lo, 1), min(hi, nk - 1)
            dot = lambda p=p: jnp.dot(a_refs[p][...], w_ref[...], preferred_element_type=F32)
            if lo == 0:
                @pl.when(k == 0)
                def _(dot=dot):
                    acc_ref[...] = dot()
            if last > first:
                @pl.when((k >= first) & (k < last))
                def _(dot=dot):
                    acc_ref[...] += dot()
            if hi == nk:
                @pl.when(k == nk - 1)
                def _(dot=dot):
                    _epilogue(acc_ref[...] + dot(), res_ref, o_ref, None)
            lo = hi

    reduce_k(ap_refs, accp_ref, rp_ref, op_ref)

    @pl.when(i == 0)
    def _():
        reduce_k(as_refs, accs_ref, rs_ref, os_ref)


def _kgrid(parts_p, parts_s, w, *, res_p=None, res_s=None):
    mp, ms = parts_p[0].shape[0], parts_s[0].shape[0]
    kdim, n = w.shape
    assert sum(a.shape[1] for a in parts_p) == kdim
    bm = _tile(mp, 1024, SUBLANES)
    bn = _tile(n, 1024, LANES)
    bk = 2048
    while any(a.shape[1] % bk for a in parts_p):
        bk //= 2
    part_blocks = tuple(a.shape[1] // bk for a in parts_p)
    nk = sum(part_blocks)
    assert nk >= 2 and bk % LANES == 0
    has_res = res_p is not None
    vmem = (2 * len(parts_p) * (bm + ms) * bk * 2 + 2 * bk * bn * 2 + (5 + 2 * has_res) * (bm + ms) * bn * 4)
    lhs_specs = lambda rows, row_of: [
        pl.BlockSpec((rows, bk), lambda j, i, k, lo=lo, blocks=blocks: (row_of(i), jnp.clip(k - lo, 0, blocks - 1)))
        for lo, blocks in zip([sum(part_blocks[:p]) for p in range(len(part_blocks))], part_blocks)]
    in_specs = lhs_specs(bm, lambda i: i) + lhs_specs(ms, lambda i: 0)
    in_specs.append(pl.BlockSpec((bk, bn), lambda j, i, k: (k, j)))
    args = [*parts_p, *parts_s, w]
    if has_res:
        in_specs += [pl.BlockSpec((bm, bn), lambda j, i, k: (i, j)), pl.BlockSpec((ms, bn), lambda j, i, k: (0, j))]
        args += [res_p, res_s]
    return pl.pallas_call(
        functools.partial(_kgrid_body, part_blocks=part_blocks, has_res=has_res),
        grid=(n // bn, mp // bm, nk),
        in_specs=in_specs,
        out_specs=[pl.BlockSpec((bm, bn), lambda j, i, k: (i, j)), pl.BlockSpec((ms, bn), lambda j, i, k: (0, j))],
        out_shape=[jax.ShapeDtypeStruct((mp, n), F32), jax.ShapeDtypeStruct((ms, n), F32)],
        scratch_shapes=[pltpu.VMEM((bm, bn), F32), pltpu.VMEM((ms, bn), F32)],
        compiler_params=_params(("parallel", "arbitrary", "arbitrary"), vmem + 2 * 1024 * 1024),
        name="kgrid",
    )(*args)


CONV_HALO = 32
CONV_ROWS = 32
CONV_STRIP = 512


def _conv_prompt_body(a_ref, g_ref, w_ref, b_ref, lng_ref, lnb_ref, v_ref, hist_ref, ubuf, sbuf, cbuf,
                      *, tile, width, nt):
    t = pl.program_id(1)
    ch = ubuf.shape[1]

    @pl.when(t == 0)
    def _():
        ubuf[0:CONV_HALO, :] = jnp.zeros((CONV_HALO, ch), F32)

    ubuf[CONV_HALO:CONV_HALO + tile, :] = a_ref[...] * _sigmoid(g_ref[...])
    off0 = CONV_HALO - (width - 1)
    srows = CONV_HALO + tile - SUBLANES

    def strip(c, carry):
        col = pl.ds(pl.multiple_of(c * CONV_STRIP, CONV_STRIP), CONV_STRIP)
        for r in range(1, SUBLANES):
            sbuf[r, 0:srows, :] = ubuf[pl.ds(r, srows), col]
        for r0 in range(0, tile, CONV_ROWS):
            acc = jnp.broadcast_to(b_ref[:, col], (CONV_ROWS, CONV_STRIP))
            for k in range(width):
                whole, r = divmod(off0 + k, SUBLANES)
                if r == 0:
                    rows = ubuf[pl.ds(r0 + off0 + k, CONV_ROWS), col]
                else:
                    rows = sbuf[r, pl.ds(r0 + whole * SUBLANES, CONV_ROWS), :]
                acc = acc + w_ref[k:k + 1, col] * rows
            cbuf[r0:r0 + CONV_ROWS, col] = acc
        return carry

    lax.fori_loop(0, ch // CONV_STRIP, strip, 0)

    y = cbuf[...]
    yc = y - jnp.mean(y, axis=-1, keepdims=True)
    yn = yc * lax.rsqrt(jnp.mean(yc * yc, axis=-1, keepdims=True) + EPS)
    v_ref[...] = _silu(yn * lng_ref[...] + lnb_ref[...]).astype(v_ref.dtype)

    ubuf[0:CONV_HALO, :] = ubuf[tile:tile + CONV_HALO, :]

    @pl.when(t == nt - 1)
    def _():
        hist_ref[0] = ubuf[0:CONV_HALO, :]


def _conv_prompt(glu, conv_w, conv_b, ln_g, ln_b, batch, seqlen):
    width, ch = conv_w.shape
    tile = min(128, seqlen)
    nt = seqlen // tile
    assert width - 1 <= CONV_HALO <= tile and seqlen % tile == 0
    assert ch % CONV_STRIP == 0 and tile % CONV_ROWS == 0
    row = lambda b, t: (b * nt + t, 0)
    vec = pl.BlockSpec((1, ch), lambda b, t: (0, 0))
    return pl.pallas_call(
        functools.partial(_conv_prompt_body, tile=tile, width=width, nt=nt),
        grid=(batch, nt),
        in_specs=[pl.BlockSpec((tile, ch), row),
                  pl.BlockSpec((tile, ch), lambda b, t: (b * nt + t, 1)),
                  pl.BlockSpec((width, ch), lambda b, t: (0, 0)),
                  vec, vec, vec],
        out_specs=[pl.BlockSpec((tile, ch), row),
                   pl.BlockSpec((1, CONV_HALO, ch), lambda b, t: (b, 0, 0))],
        out_shape=[jax.ShapeDtypeStruct((batch * seqlen, ch), BF16),
                   jax.ShapeDtypeStruct((batch, CONV_HALO, ch), F32)],
        scratch_shapes=[pltpu.VMEM((CONV_HALO + tile, ch), F32),
                        pltpu.VMEM((SUBLANES, CONV_HALO + tile, CONV_STRIP), F32),
                        pltpu.VMEM((tile, ch), F32)],
        compiler_params=_params(("arbitrary", "arbitrary"), 32 * 1024 * 1024),
        name="conv_prompt",
    )(glu, glu, conv_w, conv_b.reshape(1, ch), ln_g.reshape(1, ch), ln_b.reshape(1, ch))


def _conv_sample_body(a_ref, g_ref, st_ref, w_ref, b_ref, lng_ref, lnb_ref, v_ref, nst_ref, *, width):
    u = a_ref[...] * _sigmoid(g_ref[...])
    y = b_ref[...] + w_ref[width - 1:width, :] * u
    for k in range(width - 1):
        y = y + w_ref[k:k + 1, :] * st_ref[k]
    yc = y - jnp.mean(y, axis=-1, keepdims=True)
    yn = yc * lax.rsqrt(jnp.mean(yc * yc, axis=-1, keepdims=True) + EPS)
    v_ref[...] = _silu(yn * lng_ref[...] + lnb_ref[...]).astype(v_ref.dtype)
    for k in range(width - 2):
        nst_ref[k] = st_ref[k + 1]
    nst_ref[width - 2] = u


def _conv_sample(glu, state, conv_w, conv_b, ln_g, ln_b):
    width, ch = conv_w.shape
    nb = glu.shape[0]
    bs = _tile(nb, 2 * SUBLANES, SUBLANES)
    assert nb % bs == 0
    vec = pl.BlockSpec((1, ch), lambda i: (0, 0))
    return pl.pallas_call(
        functools.partial(_conv_sample_body, width=width),
        grid=(nb // bs,),
        in_specs=[pl.BlockSpec((bs, ch), lambda i: (i, 0)),
                  pl.BlockSpec((bs, ch), lambda i: (i, 1)),
                  pl.BlockSpec((width - 1, bs, ch), lambda i: (0, i, 0)),
                  pl.BlockSpec((width, ch), lambda i: (0, 0)),
                  vec, vec, vec],
        out_specs=[pl.BlockSpec((bs, ch), lambda i: (i, 0)),
                   pl.BlockSpec((width - 1, bs, ch), lambda i: (0, i, 0))],
        out_shape=[jax.ShapeDtypeStruct((nb, ch), BF16),
                   jax.ShapeDtypeStruct(state.shape, F32)],
        compiler_params=_params(("parallel",), 32 * 1024 * 1024),
        name="conv_sample",
    )(glu, glu, state, conv_w, conv_b.reshape(1, ch), ln_g.reshape(1, ch), ln_b.reshape(1, ch))


def _expand_heads(v, heads, head_dim):
    rows = v.shape[0]
    per_tile = LANES // head_dim
    lane = lax.broadcasted_iota(jnp.int32, (rows, LANES), 1)
    tiles = []
    for j in range(heads // per_tile):
        h0 = j * per_tile
        t = jnp.broadcast_to(v[:, h0:h0 + 1], (rows, LANES))
        for i in range(1, per_tile):
            t = jnp.where(lane >= i * head_dim, jnp.broadcast_to(v[:, h0 + i:h0 + i + 1], (rows, LANES)), t)
        tiles.append(t)
    return jnp.concatenate(tiles, axis=1)


def _pad_heads(v, groups):
    hg = v.shape[0] // groups
    return jnp.pad(v.reshape(groups, hg).astype(F32), ((0, 0), (0, LANES - hg))).reshape(1, groups * LANES)


def _group_rmsnorm_gate(y, xs, z, dskip, normg):
    y = (y + dskip * xs) * _silu(z)
    return y * lax.rsqrt(jnp.mean(y * y, axis=-1, keepdims=True) + EPS) * normg


def _ssd_prompt_body(xs_ref, z_ref, b_ref, c_ref, dt_ref, wx_ref, wb_ref, wc_ref, bx_ref, bb_ref, bc_ref,
                     dtb_ref, alog_ref, dskip_ref, normg_ref, bd_ref, rep_ref,
                     y_ref, st_ref, state, xbuf, bbuf, cbuf, *, q, nc, head_dim, cw):
    c = pl.program_id(2)
    gw = xbuf.shape[1]
    n = bbuf.shape[1]
    hg = gw // head_dim
    halo = SUBLANES

    @pl.when(c == 0)
    def _():
        state[...] = jnp.zeros(state.shape, F32)
        xbuf[0:halo, :] = jnp.zeros((halo, gw), F32)
        bbuf[0:halo, :] = jnp.zeros((halo, n), F32)
        cbuf[0:halo, :] = jnp.zeros((halo, n), F32)

    def causal_conv(raw_ref, buf, w_ref, bias_ref):
        buf[halo:halo + q, :] = raw_ref[...]
        rows = buf[...]
        acc = bias_ref[...] + w_ref[cw - 1:cw, :] * rows[halo:]
        for j in range(1, cw):
            acc = acc + w_ref[cw - 1 - j:cw - j, :] * pltpu.roll(rows, j, axis=0)[halo:]
        buf[0:halo, :] = rows[q:q + halo]
        return _silu(acc)

    xs = causal_conv(xs_ref, xbuf, wx_ref, bx_ref)
    bm = causal_conv(b_ref, bbuf, wb_ref, bb_ref)
    cm = causal_conv(c_ref, cbuf, wc_ref, bc_ref)

    dt = jax.nn.softplus(dt_ref[...] + dtb_ref[...])
    dta = dt * (-jnp.exp(alog_ref[...]))
    qi = lax.broadcasted_iota(jnp.int32, (q, q), 0)
    si = lax.broadcasted_iota(jnp.int32, (q, q), 1)
    causal = qi >= si
    tri = causal.astype(BF16)
    hi, mid, lo = _split3(dta)
    cs = (jnp.dot(tri, hi, preferred_element_type=F32) + jnp.dot(tri, mid, preferred_element_type=F32)
          + jnp.dot(tri, lo, preferred_element_type=F32))
    cs_t = cs.T
    dt_t = dt.T
    per_head = jnp.concatenate([jnp.exp(cs), dt * jnp.exp(cs[q - 1:q, :] - cs)], axis=0)
    spread = rep_ref[...]
    per_lane = sum(jnp.dot(piece, spread, preferred_element_type=F32) for piece in _split3(per_head))
    exp_cs_x = per_lane[0:q]
    xdt_end = xs * per_lane[q:2 * q]

    xs16 = xs.astype(BF16)
    cm16 = cm.astype(BF16)
    cb = lax.dot_general(cm16, bm.astype(BF16), (((1,), (1,)), ((), ())), preferred_element_type=F32)

    hb = MXU_DIM // head_dim
    slab = hb * head_dim
    cb = jnp.where(causal, cb, 0.0)
    blockdiag = bd_ref[...]
    y_parts = []
    for s in range(hg // hb):
        lhs = []
        for j in range(hb):
            h = s * hb + j
            seg = cs[:, h:h + 1] - cs_t[h:h + 1, :]
            lmat = jnp.exp(jnp.where(causal, seg, 0.0))
            lhs.append((cb * lmat * dt_t[h:h + 1, :]).astype(BF16))
        xslab = xs16[:, s * slab:(s + 1) * slab]
        rhs = jnp.concatenate([xslab] * hb, axis=0) * blockdiag
        y_parts.append(jnp.dot(jnp.concatenate(lhs, axis=1), rhs, preferred_element_type=F32))
    y = jnp.concatenate(y_parts, axis=1)

    st = state[...]
    y = y + jnp.dot(cm16, st.astype(BF16), preferred_element_type=F32) * exp_cs_x
    upd = jnp.dot(bm.T.astype(BF16), xdt_end.astype(BF16), preferred_element_type=F32)
    new_state = st * exp_cs_x[q - 1:q, :] + upd
    state[...] = new_state

    y_ref[...] = _group_rmsnorm_gate(y, xs, z_ref[...], dskip_ref[...], normg_ref[...]).astype(y_ref.dtype)

    @pl.when(c == nc - 1)
    def _():
        st_ref[0] = new_state.T


N_SSD_INPUTS = 17


def _ssd_prompt_and_casts_body(*refs, n_cast, **kw):
    ins, riders = refs[:N_SSD_INPUTS], refs[N_SSD_INPUTS:N_SSD_INPUTS + n_cast]
    outs = refs[N_SSD_INPUTS + n_cast:N_SSD_INPUTS + n_cast + 2]
    rider_outs = refs[N_SSD_INPUTS + n_cast + 2:N_SSD_INPUTS + 2 * n_cast + 2]
    scratch = refs[N_SSD_INPUTS + 2 * n_cast + 2:]
    for src, dst in zip(riders, rider_outs):
        dst[...] = src[...].astype(dst.dtype)
    _ssd_prompt_body(*ins, *outs, *scratch, **kw)


def _ssd_prompt(xs, z, bc, dt, wx, wbc, bx, bbc, dtb, alog, dskip, normg, batch, seqlen, groups, head_dim, n,
                cast_riders=()):
    m, d_ssm = xs.shape
    gw = d_ssm // groups
    hg = gw // head_dim
    cw = wx.shape[0]
    q = min(128, seqlen)
    nc = seqlen // q
    hb = MXU_DIM // head_dim
    assert seqlen % q == 0 and q % LANES == 0 and n == LANES and hg <= LANES
    assert gw % MXU_DIM == 0 and MXU_DIM % head_dim == 0 and cw - 1 <= SUBLANES
    row = lambda b, g, c: (b * nc + c, g)
    gcol = lambda b, g, c: (0, g)
    in_specs = [
        pl.BlockSpec((q, gw), row),
        pl.BlockSpec((q, gw), row),
        pl.BlockSpec((q, n), row),
        pl.BlockSpec((q, n), lambda b, g, c: (b * nc + c, groups + g)),
        pl.BlockSpec((q, LANES), row),
        pl.BlockSpec((cw, gw), gcol),
        pl.BlockSpec((cw, n), gcol),
        pl.BlockSpec((cw, n), lambda b, g, c: (0, groups + g)),
        pl.BlockSpec((1, gw), gcol),
        pl.BlockSpec((1, n), gcol),
        pl.BlockSpec((1, n), lambda b, g, c: (0, groups + g)),
        pl.BlockSpec((1, LANES), gcol),
        pl.BlockSpec((1, LANES), gcol),
        pl.BlockSpec((1, gw), gcol),
        pl.BlockSpec((1, gw), gcol),
        pl.BlockSpec((hb * q, MXU_DIM), lambda b, g, c: (0, 0)),
        pl.BlockSpec((LANES, gw), lambda b, g, c: (0, 0)),
    ]
    spread = (jnp.arange(LANES)[:, None] == (jnp.arange(gw)[None, :] // head_dim)).astype(BF16)
    blockdiag = ((jnp.arange(hb * q)[:, None] // q) == (jnp.arange(MXU_DIM)[None, :] // head_dim)).astype(BF16)
    assert len(in_specs) == N_SSD_INPUTS
    steps = batch * groups * nc
    rider_specs = []
    for r in cast_riders:
        slab = r.shape[0] // steps
        assert r.shape[0] % steps == 0 and slab % (2 * SUBLANES) == 0
        rider_specs.append(pl.BlockSpec((slab, r.shape[1]), lambda b, g, c: ((b * groups + g) * nc + c, 0)))
    return pl.pallas_call(
        functools.partial(_ssd_prompt_and_casts_body, n_cast=len(cast_riders), q=q, nc=nc, head_dim=head_dim, cw=cw),
        grid=(batch, groups, nc),
        in_specs=in_specs + rider_specs,
        out_specs=[pl.BlockSpec((q, gw), row),
                   pl.BlockSpec((1, gw, n), lambda b, g, c: (b, g, 0))] + rider_specs,
        out_shape=[jax.ShapeDtypeStruct((m, d_ssm), BF16),
                   jax.ShapeDtypeStruct((batch, d_ssm, n), F32)]
                  + [jax.ShapeDtypeStruct(r.shape, BF16) for r in cast_riders],
        scratch_shapes=[pltpu.VMEM((n, gw), F32),
                        pltpu.VMEM((SUBLANES + q, gw), F32),
                        pltpu.VMEM((SUBLANES + q, n), F32),
                        pltpu.VMEM((SUBLANES + q, n), F32)],
        compiler_params=_params(("arbitrary", "arbitrary", "arbitrary"), 32 * 1024 * 1024),
        name="ssd_prompt",
    )(xs, z, bc, bc, dt, wx, wbc, wbc, bx, bbc, bbc, dtb, alog, dskip, normg, blockdiag, spread, *cast_riders)


def _ssm_sample_pre_body(xs_ref, b_ref, c_ref, dt_ref, hx_ref, hb_ref, hc_ref, wx_ref, wb_ref, wc_ref,
                         bx_ref, bb_ref, bc_ref, dtb_ref, alog_ref,
                         xs_o, xdt_o, b_o, c_o, decay_o, *, cw, head_dim):
    def conv(new_ref, hist_ref, w_ref, bias_ref):
        acc = bias_ref[...] + w_ref[cw - 1:cw, :] * new_ref[...]
        for k in range(cw - 1):
            acc = acc + w_ref[k:k + 1, :] * hist_ref[k]
        return _silu(acc)

    xs = conv(xs_ref, hx_ref, wx_ref, bx_ref)
    b_o[...] = conv(b_ref, hb_ref, wb_ref, bb_ref)
    c_o[...] = conv(c_ref, hc_ref, wc_ref, bc_ref)
    dt = jax.nn.softplus(dt_ref[...] + dtb_ref[...])
    decay_o[...] = jnp.exp(dt * (-jnp.exp(alog_ref[...])))
    xs_o[...] = xs
    xdt_o[...] = (xs * _expand_heads(dt, xs.shape[1] // head_dim, head_dim)).astype(xdt_o.dtype)


def _ssm_sample_pre(xs, bc, dt, hist, wx, wbc, bx, bbc, dtb, alog, groups, head_dim, n):
    nb, d_ssm = xs.shape
    gw = d_ssm // groups
    hg = gw // head_dim
    cw = wx.shape[0]
    xoff = d_ssm // n
    gcol = lambda g: (0, g)
    in_specs = [
        pl.BlockSpec((nb, gw), gcol),
        pl.BlockSpec((nb, n), gcol),
        pl.BlockSpec((nb, n), lambda g: (0, groups + g)),
        pl.BlockSpec((nb, LANES), gcol),
        pl.BlockSpec((cw - 1, nb, gw), lambda g: (0, 0, g)),
        pl.BlockSpec((cw - 1, nb, n), lambda g: (0, 0, xoff + g)),
        pl.BlockSpec((cw - 1, nb, n), lambda g: (0, 0, xoff + groups + g)),
        pl.BlockSpec((cw, gw), gcol),
        pl.BlockSpec((cw, n), gcol),
        pl.BlockSpec((cw, n), lambda g: (0, groups + g)),
        pl.BlockSpec((1, gw), gcol),
        pl.BlockSpec((1, n), gcol),
        pl.BlockSpec((1, n), lambda g: (0, groups + g)),
        pl.BlockSpec((1, LANES), gcol),
        pl.BlockSpec((1, LANES), gcol),
    ]
    return pl.pallas_call(
        functools.partial(_ssm_sample_pre_body, cw=cw, head_dim=head_dim),
        grid=(groups,),
        in_specs=in_specs,
        out_specs=[pl.BlockSpec((nb, gw), gcol),
                   pl.BlockSpec((nb, gw), gcol),
                   pl.BlockSpec((nb, n), gcol),
                   pl.BlockSpec((nb, n), gcol),
                   pl.BlockSpec((nb, LANES), gcol)],
        out_shape=[jax.ShapeDtypeStruct((nb, d_ssm), F32),
                   jax.ShapeDtypeStruct((nb, d_ssm), BF16),
                   jax.ShapeDtypeStruct((nb, groups * n), F32),
                   jax.ShapeDtypeStruct((nb, groups * n), F32),
                   jax.ShapeDtypeStruct((nb, groups * LANES), F32)],
        compiler_params=_params(("parallel",), 32 * 1024 * 1024),
        name="ssm_sample_pre",
    )(xs, bc, bc, dt, hist, hist, hist, wx, wbc, wbc, bx, bbc, bbc, dtb, alog)


def _ssm_sample_state_body(decay_ref, xdt_t_ref, bg_ref, cg_ref, xs_ref, z_ref, dskip_ref, normg_ref, st_ref,
                           y_ref, nst_ref, *, groups, hg, head_dim):
    b = pl.program_id(0)
    nb = bg_ref.shape[1]
    n = bg_ref.shape[2]
    gw = hg * head_dim
    heads = groups * hg
    is_b = lax.broadcasted_iota(jnp.int32, (nb, n), 0) == b
    for g in range(groups):
        lo = g * gw
        bsel = jnp.where(is_b, bg_ref[g], 0.0).astype(BF16)
        outer = jnp.dot(xdt_t_ref[lo:lo + gw, :], bsel, preferred_element_type=F32)
        for h in range(hg):
            r0 = lo + h * head_dim
            d = decay_ref[b * heads + g * hg + h]
            nst_ref[0, r0:r0 + head_dim, :] = (st_ref[0, r0:r0 + head_dim, :] * d
                                               + outer[h * head_dim:(h + 1) * head_dim, :])
        c_row = jnp.broadcast_to(cg_ref[g, pl.ds(b, 1), :], (SUBLANES, n)).astype(BF16)
        y = lax.dot_general(c_row, nst_ref[0, lo:lo + gw, :].astype(BF16), (((1,), (1,)), ((), ())),
                            preferred_element_type=F32)[0:1, :]
        y = _group_rmsnorm_gate(y, xs_ref[pl.ds(b, 1), lo:lo + gw], z_ref[pl.ds(b, 1), lo:lo + gw],
                                dskip_ref[:, lo:lo + gw], normg_ref[:, lo:lo + gw])
        y_ref[0, :, lo:lo + gw] = y.astype(y_ref.dtype)


def _ssm_sample_state(decay, xdt_t, bg, cg, xs, z, dskip, normg, state, groups, head_dim):
    nb, d_ssm = xs.shape
    n = state.shape[-1]
    hg = d_ssm // groups // head_dim
    full2 = lambda b: (0, 0)
    full3 = lambda b: (0, 0, 0)
    return pl.pallas_call(
        functools.partial(_ssm_sample_state_body, groups=groups, hg=hg, head_dim=head_dim),
        grid=(nb,),
        in_specs=[pl.BlockSpec(memory_space=pltpu.SMEM),
                  pl.BlockSpec((d_ssm, nb), full2),
                  pl.BlockSpec((groups, nb, n), full3),
                  pl.BlockSpec((groups, nb, n), full3),
                  pl.BlockSpec((nb, d_ssm), full2),
                  pl.BlockSpec((nb, d_ssm), full2),
                  pl.BlockSpec((1, d_ssm), full2),
                  pl.BlockSpec((1, d_ssm), full2),
                  pl.BlockSpec((1, d_ssm, n), lambda b: (b, 0, 0))],
        out_specs=[pl.BlockSpec((1, 1, d_ssm), lambda b: (b, 0, 0)),
                   pl.BlockSpec((1, d_ssm, n), lambda b: (b, 0, 0))],
        out_shape=[jax.ShapeDtypeStruct((nb, 1, d_ssm), F32),
                   jax.ShapeDtypeStruct((nb, d_ssm, n), F32)],
        compiler_params=_params(("arbitrary",), 48 * 1024 * 1024),
        name="ssm_sample_state",
    )(decay, xdt_t, bg, cg, xs, z, dskip, normg, state)


def _softmax_rows(s):
    e = jnp.exp(s - jnp.max(s, axis=-1, keepdims=True))
    return e / jnp.sum(e, axis=-1, keepdims=True)


def _attn_prompt_body(q_ref, k_ref, v_ref, o_ref, *, scale):
    s = lax.dot_general(q_ref[...], k_ref[...].astype(BF16), (((1,), (1,)), ((), ())),
                        preferred_element_type=F32) * scale
    p = _softmax_rows(s)
    o_ref[...] = jnp.dot(p.astype(BF16), v_ref[...].astype(BF16), preferred_element_type=F32).astype(o_ref.dtype)


def _attn_prompt(q, k, v, batch, seqlen, n_mem, heads):
    m, d = q.shape
    hd = d // heads
    tq = min(512, seqlen)
    nq = seqlen // tq
    return pl.pallas_call(
        functools.partial(_attn_prompt_body, scale=hd ** -0.5),
        grid=(batch, heads, nq),
        in_specs=[pl.BlockSpec((tq, hd), lambda b, h, i: (b * nq + i, h)),
                  pl.BlockSpec((n_mem, hd), lambda b, h, i: (b, h)),
                  pl.BlockSpec((n_mem, hd), lambda b, h, i: (b, h))],
        out_specs=pl.BlockSpec((tq, hd), lambda b, h, i: (b * nq + i, h)),
        out_shape=jax.ShapeDtypeStruct((m, d), BF16),
        compiler_params=_params(("parallel", "parallel", "parallel"), 32 * 1024 * 1024),
        name="attn_prompt",
    )(q, k, v)


def _attn_sample_body(q_ref, k_ref, v_ref, o_ref, *, scale, chunk):
    q = q_ref[0]
    n_mem = k_ref.shape[2]
    s = jnp.concatenate(
        [jnp.sum(k_ref[0, 0, m0:m0 + chunk] * q[None], axis=-1, keepdims=True) for m0 in range(0, n_mem, chunk)],
        axis=0) * scale
    e = jnp.exp(s - jnp.max(s, axis=0, keepdims=True))
    p = e / jnp.sum(e, axis=0, keepdims=True)
    o = jnp.sum(p[0:chunk] * v_ref[0, 0, 0:chunk], axis=0)
    for m0 in range(chunk, n_mem, chunk):
        o = o + jnp.sum(p[m0:m0 + chunk] * v_ref[0, 0, m0:m0 + chunk], axis=0)
    o_ref[0] = o


def _attn_sample(q, k, v):
    _, nb, n_mem, heads, hd = k.shape
    chunk = min(32, n_mem)
    assert n_mem % chunk == 0
    kv_spec = pl.BlockSpec((1, 1, n_mem, heads, hd), lambda b: (0, b, 0, 0, 0))
    return pl.pallas_call(
        functools.partial(_attn_sample_body, scale=hd ** -0.5, chunk=chunk),
        grid=(nb,),
        in_specs=[pl.BlockSpec((1, heads, hd), lambda b: (b, 0, 0)), kv_spec, kv_spec],
        out_specs=pl.BlockSpec((1, heads, hd), lambda b: (b, 0, 0)),
        out_shape=jax.ShapeDtypeStruct((nb, heads, hd), F32),
        compiler_params=_params(("parallel",), 4 * n_mem * SUBLANES * hd * 4 + 8 * 1024 * 1024),
        name="attn_sample",
    )(q, k, v)


def kernel(x_prompt, x_sample, mem_prompt, state_conv, state_ssm_conv, state_ssm, cache_mem_k, cache_mem_v,
           g_mix, w_in, conv_w, conv_b, ln_g, ln_b, ssm_conv_w, ssm_conv_b, dt_bias, a_log, d_skip,
           ssm_norm_g, w_out, g_xattn, g_mem, w_q, w_k, w_v, w_o, g_mlp, w_up, w_down, g_final):
    depth = g_mix.shape[0]
    assert depth == 1
    batch, seqlen, d = x_prompt.shape
    nb = x_sample.shape[0]
    assert x_sample.shape[1] == 1
    d_conv = conv_w.shape[-1]
    c_xbc = ssm_conv_w.shape[-1]
    d_ssm = ssm_norm_g.shape[-1]
    heads = dt_bias.shape[-1]
    head_dim = d_ssm // heads
    n = state_ssm.shape[-1]
    groups = (c_xbc - d_ssm) // (2 * n)
    hg = heads // groups
    n_mem, xa_heads = cache_mem_k.shape[2], cache_mem_k.shape[3]
    cw = ssm_conv_w.shape[1]
    assert n == LANES and hg <= LANES

    o_z, o_x = 2 * d_conv, 2 * d_conv + d_ssm
    o_b, o_dt = o_x + d_ssm, o_x + c_xbc
    w_in_t = jnp.swapaxes(w_in, 1, 2)
    w_dt_t = jnp.pad(w_in_t[0, o_dt:].reshape(groups, hg, d), ((0, 0), (0, LANES - hg), (0, 0)))
    w_dt_t = w_dt_t.reshape(1, groups * LANES, d)
    wx, wbc = ssm_conv_w[0][:, :d_ssm], ssm_conv_w[0][:, d_ssm:]
    bx, bbc = ssm_conv_b[0][:d_ssm].reshape(1, d_ssm), ssm_conv_b[0][d_ssm:].reshape(1, c_xbc - d_ssm)
    dtb, alog = _pad_heads(dt_bias[0], groups), _pad_heads(a_log[0], groups)
    dskip = jnp.repeat(d_skip[0].astype(F32), head_dim).reshape(1, d_ssm)
    normg = ssm_norm_g[0].reshape(1, d_ssm)

    mp = batch * seqlen
    xp, xsm = x_prompt.reshape(mp, d), x_sample.reshape(nb, d)
    norm2 = lambda a_p, a_s, g, dtype: (_rmsnorm(a_p, g, dtype), _rmsnorm(a_s, g, dtype))
    in_proj = functools.partial(_proj, *norm2(xp, xsm, g_mix[0], BF16), out_dtype=F32, w_rows_are_outputs=True)
    glu, glu_s = in_proj(w_in_t, col0=0, ncols=o_z)
    z, z_s = in_proj(w_in_t, col0=o_z, ncols=d_ssm)
    xs, xs_s = in_proj(w_in_t, col0=o_x, ncols=d_ssm)
    bc, bc_s = in_proj(w_in_t, col0=o_b, ncols=o_dt - o_b)
    dt, dt_s = in_proj(w_dt_t)

    v_p, conv_hist = _conv_prompt(glu, conv_w[0], conv_b[0], ln_g[0], ln_b[0], batch, seqlen)
    y_p, st_p, w_out16, w_down16 = _ssd_prompt(xs, z, bc, dt, wx, wbc, bx, bbc, dtb, alog, dskip, normg,
                                               batch, seqlen, groups, head_dim, n,
                                               cast_riders=(w_out[0], w_down[0]))
    keep = conv_w.shape[1] - 1
    new_conv_p = conv_hist[:, CONV_HALO - keep:, :]
    tail = lambda a: a.reshape(batch, seqlen, -1)[:, seqlen - (cw - 1):, :]
    new_ssm_conv_p = jnp.concatenate([tail(xs), tail(bc)], axis=-1)

    v_s, new_conv_t = _conv_sample(glu_s, jnp.swapaxes(state_conv[0], 0, 1), conv_w[0], conv_b[0], ln_g[0], ln_b[0])
    new_conv_s = jnp.swapaxes(new_conv_t, 0, 1)[None]
    hist = jnp.swapaxes(state_ssm_conv[0], 0, 1)
    xs_c, xdt, b_c, c_c, decay = _ssm_sample_pre(xs_s, bc_s, dt_s, hist, wx, wbc, bx, bbc, dtb, alog,
                                                  groups, head_dim, n)
    decay_flat = decay.reshape(nb, groups, LANES)[:, :, :hg].reshape(nb * heads)
    to_groups = lambda a: jnp.swapaxes(a.reshape(nb, groups, n), 0, 1)
    y_s, st_s = _ssm_sample_state(decay_flat, xdt.T, to_groups(b_c), to_groups(c_c), xs_c, z_s, dskip, normg,
                                  state_ssm[0].reshape(nb, d_ssm, n), groups, head_dim)
    new_ssm_conv_s = jnp.swapaxes(
        jnp.concatenate([hist[1:], jnp.concatenate([xs_s, bc_s], axis=-1)[None]], axis=0), 0, 1)

    x1, x1_s = _kgrid([v_p, y_p], [v_s, y_s.reshape(nb, d_ssm).astype(BF16)], w_out16,
                      res_p=xp, res_s=xsm)
    q, q_s = _proj(*norm2(x1, x1_s, g_xattn[0], BF16), w_q, out_dtype=BF16)
    mem_n = _rmsnorm(mem_prompt.reshape(batch * n_mem, d), g_mem[0], BF16)
    k_p, _ = _proj(mem_n, None, w_k, out_dtype=F32)
    v_mem_p, _ = _proj(mem_n, None, w_v, out_dtype=F32)
    o = _attn_prompt(q, k_p, v_mem_p, batch, seqlen, n_mem, xa_heads)
    o_s = _attn_sample(q_s.astype(F32).reshape(nb, xa_heads, d // xa_heads), cache_mem_k,
                       cache_mem_v).reshape(nb, d).astype(BF16)
    x2, x2_s = _proj(o, o_s, w_o, out_dtype=F32, res_p=x1, res_s=x1_s)
    up, up_s = _proj(*norm2(x2, x2_s, g_mlp[0], BF16), w_up, out_dtype=BF16, act="relu2")
    x3, x3_s = _kgrid([up], [up_s], w_down16, res_p=x2, res_s=x2_s)
    y_prompt, y_sample = norm2(x3, x3_s, g_final, F32)

    kv_shape = (1, batch, n_mem, xa_heads, d // xa_heads)
    return (y_prompt.reshape(batch, seqlen, d), y_sample.reshape(nb, 1, d),
            new_conv_p[None], new_ssm_conv_p[None], st_p.reshape(1, batch, heads, head_dim, n),
            k_p.reshape(kv_shape), v_mem_p.reshape(kv_shape),
            new_conv_s, new_ssm_conv_s[None], st_s.reshape(1, nb, heads, head_dim, n))
```

```python
import functools

import jax
import jax.numpy as jnp
from jax import lax
from jax.experimental import pallas as pl
from jax.experimental.pallas import tpu as pltpu

F32 = jnp.float32
BF16 = jnp.bfloat16
EPS = 1e-5
LANES = 128
SUBLANES = 8
MXU_DIM = 256
VMEM_CAP = 62 * 1024 * 1024
PROJ_VMEM_SLACK = 4 * 1024 * 1024


def _params(sem, vmem_bytes):
    return pltpu.CompilerParams(dimension_semantics=sem,
                                vmem_limit_bytes=int(min(VMEM_CAP, max(vmem_bytes, 16 * 1024 * 1024))))


def _sigmoid(x):
    return 0.5 * jnp.tanh(0.5 * x) + 0.5


def _silu(x):
    h = 0.5 * x
    return h * jnp.tanh(h) + h


def _split3(x):
    hi = x.astype(BF16)
    r1 = x - hi.astype(F32)
    mid = r1.astype(BF16)
    return hi, mid, (r1 - mid.astype(F32)).astype(BF16)


def _tile(dim, cap, unit):
    if dim <= cap:
        return dim
    best = max(t for t in range(unit, cap + 1, unit) if dim % t == 0)
    return best


def _rmsnorm_body(x_ref, g_ref, o_ref):
    x = x_ref[...]
    ms = jnp.mean(x * x, axis=-1, keepdims=True)
    o_ref[...] = (x * lax.rsqrt(ms + EPS) * g_ref[...]).astype(o_ref.dtype)


def _rmsnorm(x, g, out_dtype, bm=256):
    m, d = x.shape
    bm = min(bm, m)
    return pl.pallas_call(
        _rmsnorm_body,
        grid=(m // bm,),
        in_specs=[pl.BlockSpec((bm, d), lambda i: (i, 0)),
                  pl.BlockSpec((1, d), lambda i: (0, 0))],
        out_specs=pl.BlockSpec((bm, d), lambda i: (i, 0)),
        out_shape=jax.ShapeDtypeStruct((m, d), out_dtype),
        compiler_params=_params(("parallel",), 6 * bm * d * 4),
        name="rmsnorm",
    )(x, g.reshape(1, d).astype(F32))


def _epilogue(acc, res_ref, o_ref, act):
    if act == "relu2":
        acc = jnp.square(jnp.maximum(acc, 0.0))
    if res_ref is not None:
        acc = res_ref[...] + acc
    o_ref[...] = acc.astype(o_ref.dtype)


def _proj_body(*refs, act, has_s, has_res, has_rider, w_rows_are_outputs, n_panels):
    if w_rows_are_outputs:
        mm = lambda a, w: lax.dot_general(a, w, (((1,), (1,)), ((), ())), preferred_element_type=F32)
    else:
        mm = lambda a, w: jnp.dot(a, w, preferred_element_type=F32)
    it = iter(refs)
    ap_ref = next(it)
    as_ref = next(it) if has_s else None
    w_ref = next(it)
    rp_ref = next(it) if has_res else None
    rs_ref = next(it) if has_res and has_s else None
    rider_ref = next(it) if has_rider else None
    op_ref = next(it)
    os_ref = next(it) if has_s else None
    rider_out_ref = next(it) if has_rider else None
    w16 = next(it)
    p, i = pl.program_id(0), pl.program_id(1)
    chunk = w_ref.shape[0]
    if has_rider:
        rider_out_ref[...] = rider_ref[...].astype(rider_out_ref.dtype)

    @pl.when(p < n_panels)
    def _():
        w16[p % 2, pl.ds(pl.multiple_of(i * chunk, chunk), chunk), :] = w_ref[...].astype(BF16)

    @pl.when(p >= 1)
    def _():
        panel = w16.at[(p - 1) % 2]
        if has_s:
            @pl.when(i == 0)
            def _():
                _epilogue(mm(as_ref[...], panel[...]), rs_ref, os_ref, act)
        _epilogue(mm(ap_ref[...], panel[...]), rp_ref, op_ref, act)


def _proj(a_p, a_s, w, *, out_dtype, col0=0, ncols=None, act=None, res_p=None, res_s=None, w_rows_are_outputs=False,
          cast_rider=None):
    mp, kdim = a_p.shape
    ncols = w.shape[1 if w_rows_are_outputs else 2] - col0 if ncols is None else ncols
    has_s, has_res = a_s is not None, res_p is not None
    osz = jnp.dtype(out_dtype).itemsize
    ms = a_s.shape[0] if has_s else 0

    def footprint(bm, bn):
        return (2 * kdim * bn * 2 + 2 * kdim * bn * 4 // (mp // bm) + 2 * (bm + ms) * kdim * 2
                + (2 * osz + 2 * 4 * has_res + 4) * (bm + ms) * bn)

    bm = _tile(mp, 1024, SUBLANES)
    ni = mp // bm
    for bn_cap in (1024, 512, 256):
        bn = max(t for t in range(LANES, bn_cap + 1, LANES) if col0 % t == 0 and ncols % t == 0)
        if footprint(bm, bn) + PROJ_VMEM_SLACK <= VMEM_CAP:
            break
    c0, nj = col0 // bn, ncols // bn
    vmem = footprint(bm, bn) + PROJ_VMEM_SLACK
    row = lambda p, i: jnp.where(p == 0, 0, i)
    col = lambda p: jnp.maximum(p - 1, 0)
    nxt = lambda p: jnp.minimum(p, nj - 1)
    in_specs = [pl.BlockSpec((bm, kdim), lambda p, i: (row(p, i), 0))]
    args = [a_p]
    out_specs = [pl.BlockSpec((bm, bn), lambda p, i: (row(p, i), col(p)))]
    out_shape = [jax.ShapeDtypeStruct((mp, ncols), out_dtype)]
    if has_s:
        in_specs.append(pl.BlockSpec((ms, kdim), lambda p, i: (0, 0)))
        args.append(a_s)
        out_specs.append(pl.BlockSpec((ms, bn), lambda p, i: (0, col(p))))
        out_shape.append(jax.ShapeDtypeStruct((ms, ncols), out_dtype))
    if w_rows_are_outputs:
        chunk = bn // ni
        in_specs.append(pl.BlockSpec((None, chunk, kdim), lambda p, i: (0, (c0 + nxt(p)) * ni + i, 0)))
    else:
        chunk = kdim // ni
        in_specs.append(pl.BlockSpec((None, chunk, bn), lambda p, i: (0, i, c0 + nxt(p))))
    assert chunk * ni == (bn if w_rows_are_outputs else kdim) and chunk % (2 * SUBLANES) == 0
    args.append(w)
    if has_res:
        in_specs.append(pl.BlockSpec((bm, bn), lambda p, i: (row(p, i), col(p))))
        args.append(res_p)
        if has_s:
            in_specs.append(pl.BlockSpec((ms, bn), lambda p, i: (0, col(p))))
            args.append(res_s)
    if cast_rider is not None:
        slab = cast_rider.shape[0] // (nj * ni)
        assert slab * nj * ni == cast_rider.shape[0] and slab % (2 * SUBLANES) == 0
        rider_spec = pl.BlockSpec((slab, cast_rider.shape[1]), lambda p, i: (col(p) * ni + i, 0))
        in_specs.append(rider_spec)
        args.append(cast_rider)
        out_specs.append(rider_spec)
        out_shape.append(jax.ShapeDtypeStruct(cast_rider.shape, BF16))
        vmem += 2 * slab * cast_rider.shape[1] * (4 + 2)
    out = pl.pallas_call(
        functools.partial(_proj_body, act=act, has_s=has_s, has_res=has_res, has_rider=cast_rider is not None,
                          w_rows_are_outputs=w_rows_are_outputs, n_panels=nj),
        grid=(nj + 1, ni),
        in_specs=in_specs,
        out_specs=out_specs,
        out_shape=out_shape,
        scratch_shapes=[pltpu.VMEM((2, bn, kdim) if w_rows_are_outputs else (2, kdim, bn), BF16)],
        compiler_params=_params(("arbitrary", "arbitrary"), vmem),
        name="proj",
    )(*args)
    main = (out[0], out[1]) if has_s else (out[0], None)
    return main if cast_rider is None else (*main, out[-1])


def _kgrid_body(*refs, part_blocks, has_res):
    nparts = len(part_blocks)
    nk = sum(part_blocks)
    it = iter(refs)
    ap_refs = [next(it) for _ in range(nparts)]
    as_refs = [next(it) for _ in range(nparts)]
    w_ref = next(it)
    rp_ref = next(it) if has_res else None
    rs_ref = next(it) if has_res else None
    op_ref, os_ref, accp_ref, accs_ref = next(it), next(it), next(it), next(it)
    i, k = pl.program_id(1), pl.program_id(2)

    def reduce_k(a_refs, acc_ref, res_ref, o_ref):
        lo = 0
        for p, blocks in enumerate(part_blocks):
            hi = lo + blocks
            first, last = max(lo, 1), min(hi, nk - 1)
            dot = lambda p=p: jnp.dot(a_refs[p][...], w_ref[...], preferred_element_type=F32)
            if lo == 0:
                @pl.when(k == 0)
                def _(dot=dot):
                    acc_ref[...] = dot()
            if last > first:
                @pl.when((k >= first) & (k < last))
                def _(dot=dot):
                    acc_ref[...] += dot()
            if hi == nk:
                @pl.when(k == nk - 1)
                def _(dot=dot):
                    _epilogue(acc_ref[...] + dot(), res_ref, o_ref, None)
            lo = hi

    reduce_k(ap_refs, accp_ref, rp_ref, op_ref)

    @pl.when(i == 0)
    def _():
        reduce_k(as_refs, accs_ref, rs_ref, os_ref)


def _kgrid(parts_p, parts_s, w, *, res_p=None, res_s=None):
    mp, ms = parts_p[0].shape[0], parts_s[0].shape[0]
    kdim, n = w.shape
    assert sum(a.shape[1] for a in parts_p) == kdim
    bm = _tile(mp, 1024, SUBLANES)
    bn = _tile(n, 1024, LANES)
    has_res = res_p is not None
    footprint = lambda bk: (2 * len(parts_p) * (bm + ms) * bk * 2 + 2 * bk * bn * 2
                            + (4 + 2 * has_res) * (bm + ms) * bn * 4)
    bk = 4096
    while any(a.shape[1] % bk for a in parts_p) or kdim // bk < 2 or footprint(bk) + PROJ_VMEM_SLACK > VMEM_CAP:
        bk //= 2
    part_blocks = tuple(a.shape[1] // bk for a in parts_p)
    nk = sum(part_blocks)
    assert nk >= 2 and bk % LANES == 0
    vmem = footprint(bk) + PROJ_VMEM_SLACK
    lhs_specs = lambda rows, row_of: [
        pl.BlockSpec((rows, bk), lambda j, i, k, lo=lo, blocks=blocks: (row_of(i), jnp.clip(k - lo, 0, blocks - 1)))
        for lo, blocks in zip([sum(part_blocks[:p]) for p in range(len(part_blocks))], part_blocks)]
    in_specs = lhs_specs(bm, lambda i: i) + lhs_specs(ms, lambda i: 0)
    in_specs.append(pl.BlockSpec((bk, bn), lambda j, i, k: (k, j)))
    args = [*parts_p, *parts_s, w]
    if has_res:
        in_specs += [pl.BlockSpec((bm, bn), lambda j, i, k: (i, j)), pl.BlockSpec((ms, bn), lambda j, i, k: (0, j))]
        args += [res_p, res_s]
    return pl.pallas_call(
        functools.partial(_kgrid_body, part_blocks=part_blocks, has_res=has_res),
        grid=(n // bn, mp // bm, nk),
        in_specs=in_specs,
        out_specs=[pl.BlockSpec((bm, bn), lambda j, i, k: (i, j)), pl.BlockSpec((ms, bn), lambda j, i, k: (0, j))],
        out_shape=[jax.ShapeDtypeStruct((mp, n), F32), jax.ShapeDtypeStruct((ms, n), F32)],
        scratch_shapes=[pltpu.VMEM((bm, bn), F32), pltpu.VMEM((ms, bn), F32)],
        compiler_params=_params(("parallel", "arbitrary", "arbitrary"), vmem),
        name="kgrid",
    )(*args)


CONV_HALO = 32
CONV_ROWS = 32
CONV_STRIP = 512


def _conv_prompt_body(a_ref, g_ref, w_ref, b_ref, lng_ref, lnb_ref, v_ref, hist_ref, ubuf, sbuf, cbuf,
                      *, tile, width, nt):
    t = pl.program_id(1)
    ch = ubuf.shape[1]

    @pl.when(t == 0)
    def _():
        ubuf[0:CONV_HALO, :] = jnp.zeros((CONV_HALO, ch), F32)

    ubuf[CONV_HALO:CONV_HALO + tile, :] = a_ref[...] * _sigmoid(g_ref[...])
    off0 = CONV_HALO - (width - 1)
    srows = CONV_HALO + tile - SUBLANES

    def strip(c, carry):
        col = pl.ds(pl.multiple_of(c * CONV_STRIP, CONV_STRIP), CONV_STRIP)
        for r in range(1, SUBLANES):
            sbuf[r, 0:srows, :] = ubuf[pl.ds(r, srows), col]
        for r0 in range(0, tile, CONV_ROWS):
            acc = jnp.broadcast_to(b_ref[:, col], (CONV_ROWS, CONV_STRIP))
            for k in range(width):
                whole, r = divmod(off0 + k, SUBLANES)
                if r == 0:
                    rows = ubuf[pl.ds(r0 + off0 + k, CONV_ROWS), col]
                else:
                    rows = sbuf[r, pl.ds(r0 + whole * SUBLANES, CONV_ROWS), :]
                acc = acc + w_ref[k:k + 1, col] * rows
            cbuf[r0:r0 + CONV_ROWS, col] = acc
        return carry

    lax.fori_loop(0, ch // CONV_STRIP, strip, 0)

    y = cbuf[...]
    yc = y - jnp.mean(y, axis=-1, keepdims=True)
    yn = yc * lax.rsqrt(jnp.mean(yc * yc, axis=-1, keepdims=True) + EPS)
    v_ref[...] = _silu(yn * lng_ref[...] + lnb_ref[...]).astype(v_ref.dtype)

    ubuf[0:CONV_HALO, :] = ubuf[tile:tile + CONV_HALO, :]

    @pl.when(t == nt - 1)
    def _():
        hist_ref[0] = ubuf[0:CONV_HALO, :]


def _conv_prompt(glu, conv_w, conv_b, ln_g, ln_b, batch, seqlen):
    width, ch = conv_w.shape
    tile = min(128, seqlen)
    nt = seqlen // tile
    assert width - 1 <= CONV_HALO <= tile and seqlen % tile == 0
    assert ch % CONV_STRIP == 0 and tile % CONV_ROWS == 0
    row = lambda b, t: (b * nt + t, 0)
    vec = pl.BlockSpec((1, ch), lambda b, t: (0, 0))
    return pl.pallas_call(
        functools.partial(_conv_prompt_body, tile=tile, width=width, nt=nt),
        grid=(batch, nt),
        in_specs=[pl.BlockSpec((tile, ch), row),
                  pl.BlockSpec((tile, ch), lambda b, t: (b * nt + t, 1)),
                  pl.BlockSpec((width, ch), lambda b, t: (0, 0)),
                  vec, vec, vec],
        out_specs=[pl.BlockSpec((tile, ch), row),
                   pl.BlockSpec((1, CONV_HALO, ch), lambda b, t: (b, 0, 0))],
        out_shape=[jax.ShapeDtypeStruct((batch * seqlen, ch), BF16),
                   jax.ShapeDtypeStruct((batch, CONV_HALO, ch), F32)],
        scratch_shapes=[pltpu.VMEM((CONV_HALO + tile, ch), F32),
                        pltpu.VMEM((SUBLANES, CONV_HALO + tile, CONV_STRIP), F32),
                        pltpu.VMEM((tile, ch), F32)],
        compiler_params=_params(("arbitrary", "arbitrary"), 32 * 1024 * 1024),
        name="conv_prompt",
    )(glu, glu, conv_w, conv_b.reshape(1, ch), ln_g.reshape(1, ch), ln_b.reshape(1, ch))


def _conv_sample_body(a_ref, g_ref, st_ref, w_ref, b_ref, lng_ref, lnb_ref, v_ref, nst_ref, *, width):
    u = a_ref[...] * _sigmoid(g_ref[...])
    y = b_ref[...] + w_ref[width - 1:width, :] * u
    for k in range(width - 1):
        y = y + w_ref[k:k + 1, :] * st_ref[k]
    yc = y - jnp.mean(y, axis=-1, keepdims=True)
    yn = yc * lax.rsqrt(jnp.mean(yc * yc, axis=-1, keepdims=True) + EPS)
    v_ref[...] = _silu(yn * lng_ref[...] + lnb_ref[...]).astype(v_ref.dtype)
    for k in range(width - 2):
        nst_ref[k] = st_ref[k + 1]
    nst_ref[width - 2] = u


def _conv_sample(glu, state, conv_w, conv_b, ln_g, ln_b):
    width, ch = conv_w.shape
    nb = glu.shape[0]
    bs = _tile(nb, 2 * SUBLANES, SUBLANES)
    assert nb % bs == 0
    vec = pl.BlockSpec((1, ch), lambda i: (0, 0))
    return pl.pallas_call(
        functools.partial(_conv_sample_body, width=width),
        grid=(nb // bs,),
        in_specs=[pl.BlockSpec((bs, ch), lambda i: (i, 0)),
                  pl.BlockSpec((bs, ch), lambda i: (i, 1)),
                  pl.BlockSpec((width - 1, bs, ch), lambda i: (0, i, 0)),
                  pl.BlockSpec((width, ch), lambda i: (0, 0)),
                  vec, vec, vec],
        out_specs=[pl.BlockSpec((bs, ch), lambda i: (i, 0)),
                   pl.BlockSpec((width - 1, bs, ch), lambda i: (0, i, 0))],
        out_shape=[jax.ShapeDtypeStruct((nb, ch), BF16),
                   jax.ShapeDtypeStruct(state.shape, F32)],
        compiler_params=_params(("parallel",), 32 * 1024 * 1024),
        name="conv_sample",
    )(glu, glu, state, conv_w, conv_b.reshape(1, ch), ln_g.reshape(1, ch), ln_b.reshape(1, ch))


def _expand_heads(v, heads, head_dim):
    rows = v.shape[0]
    per_tile = LANES // head_dim
    lane = lax.broadcasted_iota(jnp.int32, (rows, LANES), 1)
    tiles = []
    for j in range(heads // per_tile):
        h0 = j * per_tile
        t = jnp.broadcast_to(v[:, h0:h0 + 1], (rows, LANES))
        for i in range(1, per_tile):
            t = jnp.where(lane >= i * head_dim, jnp.broadcast_to(v[:, h0 + i:h0 + i + 1], (rows, LANES)), t)
        tiles.append(t)
    return jnp.concatenate(tiles, axis=1)


def _pad_heads(v, groups):
    hg = v.shape[0] // groups
    return jnp.pad(v.reshape(groups, hg).astype(F32), ((0, 0), (0, LANES - hg))).reshape(1, groups * LANES)


def _group_rmsnorm_gate(y, xs, z, dskip, normg):
    y = (y + dskip * xs) * _silu(z)
    return y * lax.rsqrt(jnp.mean(y * y, axis=-1, keepdims=True) + EPS) * normg


def _ssd_prompt_body(xs_ref, z_ref, b_ref, c_ref, dt_ref, wx_ref, wb_ref, wc_ref, bx_ref, bb_ref, bc_ref,
                     dtb_ref, alog_ref, dskip_ref, normg_ref, bd_ref, rep_ref,
                     y_ref, st_ref, state, xbuf, bbuf, cbuf, *, q, nc, head_dim, cw):
    c = pl.program_id(2)
    gw = xbuf.shape[1]
    n = bbuf.shape[1]
    hg = gw // head_dim
    halo = SUBLANES

    @pl.when(c == 0)
    def _():
        state[...] = jnp.zeros(state.shape, F32)
        xbuf[0:halo, :] = jnp.zeros((halo, gw), F32)
        bbuf[0:halo, :] = jnp.zeros((halo, n), F32)
        cbuf[0:halo, :] = jnp.zeros((halo, n), F32)

    def causal_conv(raw_ref, buf, w_ref, bias_ref):
        buf[halo:halo + q, :] = raw_ref[...]
        rows = buf[...]
        acc = bias_ref[...] + w_ref[cw - 1:cw, :] * rows[halo:]
        for j in range(1, cw):
            acc = acc + w_ref[cw - 1 - j:cw - j, :] * pltpu.roll(rows, j, axis=0)[halo:]
        buf[0:halo, :] = rows[q:q + halo]
        return _silu(acc)

    xs = causal_conv(xs_ref, xbuf, wx_ref, bx_ref)
    bm = causal_conv(b_ref, bbuf, wb_ref, bb_ref)
    cm = causal_conv(c_ref, cbuf, wc_ref, bc_ref)

    dt = jax.nn.softplus(dt_ref[...] + dtb_ref[...])
    dta = dt * (-jnp.exp(alog_ref[...]))
    qi = lax.broadcasted_iota(jnp.int32, (q, q), 0)
    si = lax.broadcasted_iota(jnp.int32, (q, q), 1)
    causal = qi >= si
    tri = causal.astype(BF16)
    hi, mid, lo = _split3(dta)
    cs = (jnp.dot(tri, hi, preferred_element_type=F32) + jnp.dot(tri, mid, preferred_element_type=F32)
          + jnp.dot(tri, lo, preferred_element_type=F32))
    cs_t = cs.T
    dt_t = dt.T
    per_head = jnp.concatenate([jnp.exp(cs), dt * jnp.exp(cs[q - 1:q, :] - cs)], axis=0)
    spread = rep_ref[...]
    per_lane = sum(jnp.dot(piece, spread, preferred_element_type=F32) for piece in _split3(per_head))
    exp_cs_x = per_lane[0:q]
    xdt_end = xs * per_lane[q:2 * q]

    xs16 = xs.astype(BF16)
    cm16 = cm.astype(BF16)
    cb = lax.dot_general(cm16, bm.astype(BF16), (((1,), (1,)), ((), ())), preferred_element_type=F32)

    hb = MXU_DIM // head_dim
    slab = hb * head_dim
    cb = jnp.where(causal, cb, 0.0)
    blockdiag = bd_ref[...]
    y_parts = []
    for s in range(hg // hb):
        lhs = []
        for j in range(hb):
            h = s * hb + j
            seg = cs[:, h:h + 1] - cs_t[h:h + 1, :]
            lmat = jnp.exp(jnp.where(causal, seg, 0.0))
            lhs.append((cb * lmat * dt_t[h:h + 1, :]).astype(BF16))
        xslab = xs16[:, s * slab:(s + 1) * slab]
        rhs = jnp.concatenate([xslab] * hb, axis=0) * blockdiag
        y_parts.append(jnp.dot(jnp.concatenate(lhs, axis=1), rhs, preferred_element_type=F32))
    y = jnp.concatenate(y_parts, axis=1)

    st = state[...]
    y = y + jnp.dot(cm16, st.astype(BF16), preferred_element_type=F32) * exp_cs_x
    upd = jnp.dot(bm.T.astype(BF16), xdt_end.astype(BF16), preferred_element_type=F32)
    new_state = st * exp_cs_x[q - 1:q, :] + upd
    state[...] = new_state

    y_ref[...] = _group_rmsnorm_gate(y, xs, z_ref[...], dskip_ref[...], normg_ref[...]).astype(y_ref.dtype)

    @pl.when(c == nc - 1)
    def _():
        st_ref[0] = new_state.T


N_SSD_INPUTS = 17


def _ssd_prompt_and_casts_body(*refs, n_cast, **kw):
    ins, riders = refs[:N_SSD_INPUTS], refs[N_SSD_INPUTS:N_SSD_INPUTS + n_cast]
    outs = refs[N_SSD_INPUTS + n_cast:N_SSD_INPUTS + n_cast + 2]
    rider_outs = refs[N_SSD_INPUTS + n_cast + 2:N_SSD_INPUTS + 2 * n_cast + 2]
    scratch = refs[N_SSD_INPUTS + 2 * n_cast + 2:]
    for src, dst in zip(riders, rider_outs):
        dst[...] = src[...].astype(dst.dtype)
    _ssd_prompt_body(*ins, *outs, *scratch, **kw)


def _ssd_prompt(xs, z, bc, dt, wx, wbc, bx, bbc, dtb, alog, dskip, normg, batch, seqlen, groups, head_dim, n,
                cast_riders=()):
    m, d_ssm = xs.shape
    gw = d_ssm // groups
    hg = gw // head_dim
    cw = wx.shape[0]
    q = min(128, seqlen)
    nc = seqlen // q
    hb = MXU_DIM // head_dim
    assert seqlen % q == 0 and q % LANES == 0 and n == LANES and hg <= LANES
    assert gw % MXU_DIM == 0 and MXU_DIM % head_dim == 0 and cw - 1 <= SUBLANES
    row = lambda b, g, c: (b * nc + c, g)
    gcol = lambda b, g, c: (0, g)
    in_specs = [
        pl.BlockSpec((q, gw), row),
        pl.BlockSpec((q, gw), row),
        pl.BlockSpec((q, n), row),
        pl.BlockSpec((q, n), lambda b, g, c: (b * nc + c, groups + g)),
        pl.BlockSpec((q, LANES), row),
        pl.BlockSpec((cw, gw), gcol),
        pl.BlockSpec((cw, n), gcol),
        pl.BlockSpec((cw, n), lambda b, g, c: (0, groups + g)),
        pl.BlockSpec((1, gw), gcol),
        pl.BlockSpec((1, n), gcol),
        pl.BlockSpec((1, n), lambda b, g, c: (0, groups + g)),
        pl.BlockSpec((1, LANES), gcol),
        pl.BlockSpec((1, LANES), gcol),
        pl.BlockSpec((1, gw), gcol),
        pl.BlockSpec((1, gw), gcol),
        pl.BlockSpec((hb * q, MXU_DIM), lambda b, g, c: (0, 0)),
        pl.BlockSpec((LANES, gw), lambda b, g, c: (0, 0)),
    ]
    spread = (jnp.arange(LANES)[:, None] == (jnp.arange(gw)[None, :] // head_dim)).astype(BF16)
    blockdiag = ((jnp.arange(hb * q)[:, None] // q) == (jnp.arange(MXU_DIM)[None, :] // head_dim)).astype(BF16)
    assert len(in_specs) == N_SSD_INPUTS
    steps = batch * groups * nc
    rider_specs = []
    for r in cast_riders:
        slab = r.shape[0] // steps
        assert r.shape[0] % steps == 0 and slab % (2 * SUBLANES) == 0
        rider_specs.append(pl.BlockSpec((slab, r.shape[1]), lambda b, g, c: ((b * groups + g) * nc + c, 0)))
    return pl.pallas_call(
        functools.partial(_ssd_prompt_and_casts_body, n_cast=len(cast_riders), q=q, nc=nc, head_dim=head_dim, cw=cw),
        grid=(batch, groups, nc),
        in_specs=in_specs + rider_specs,
        out_specs=[pl.BlockSpec((q, gw), row),
                   pl.BlockSpec((1, gw, n), lambda b, g, c: (b, g, 0))] + rider_specs,
        out_shape=[jax.ShapeDtypeStruct((m, d_ssm), BF16),
                   jax.ShapeDtypeStruct((batch, d_ssm, n), F32)]
                  + [jax.ShapeDtypeStruct(r.shape, BF16) for r in cast_riders],
        scratch_shapes=[pltpu.VMEM((n, gw), F32),
                        pltpu.VMEM((SUBLANES + q, gw), F32),
                        pltpu.VMEM((SUBLANES + q, n), F32),
                        pltpu.VMEM((SUBLANES + q, n), F32)],
        compiler_params=_params(("arbitrary", "arbitrary", "arbitrary"), 32 * 1024 * 1024),
        name="ssd_prompt",
    )(xs, z, bc, bc, dt, wx, wbc, wbc, bx, bbc, bbc, dtb, alog, dskip, normg, blockdiag, spread, *cast_riders)


def _ssm_sample_pre_body(xs_ref, b_ref, c_ref, dt_ref, hx_ref, hb_ref, hc_ref, wx_ref, wb_ref, wc_ref,
                         bx_ref, bb_ref, bc_ref, dtb_ref, alog_ref,
                         xs_o, xdt_o, b_o, c_o, decay_o, *, cw, head_dim):
    def conv(new_ref, hist_ref, w_ref, bias_ref):
        acc = bias_ref[...] + w_ref[cw - 1:cw, :] * new_ref[...]
        for k in range(cw - 1):
            acc = acc + w_ref[k:k + 1, :] * hist_ref[k]
        return _silu(acc)

    xs = conv(xs_ref, hx_ref, wx_ref, bx_ref)
    b_o[...] = conv(b_ref, hb_ref, wb_ref, bb_ref)
    c_o[...] = conv(c_ref, hc_ref, wc_ref, bc_ref)
    dt = jax.nn.softplus(dt_ref[...] + dtb_ref[...])
    decay_o[...] = jnp.exp(dt * (-jnp.exp(alog_ref[...])))
    xs_o[...] = xs
    xdt_o[...] = (xs * _expand_heads(dt, xs.shape[1] // head_dim, head_dim)).astype(xdt_o.dtype)


def _ssm_sample_pre(xs, bc, dt, hist, wx, wbc, bx, bbc, dtb, alog, groups, head_dim, n):
    nb, d_ssm = xs.shape
    gw = d_ssm // groups
    hg = gw // head_dim
    cw = wx.shape[0]
    xoff = d_ssm // n
    gcol = lambda g: (0, g)
    in_specs = [
        pl.BlockSpec((nb, gw), gcol),
        pl.BlockSpec((nb, n), gcol),
        pl.BlockSpec((nb, n), lambda g: (0, groups + g)),
        pl.BlockSpec((nb, LANES), gcol),
        pl.BlockSpec((cw - 1, nb, gw), lambda g: (0, 0, g)),
        pl.BlockSpec((cw - 1, nb, n), lambda g: (0, 0, xoff + g)),
        pl.BlockSpec((cw - 1, nb, n), lambda g: (0, 0, xoff + groups + g)),
        pl.BlockSpec((cw, gw), gcol),
        pl.BlockSpec((cw, n), gcol),
        pl.BlockSpec((cw, n), lambda g: (0, groups + g)),
        pl.BlockSpec((1, gw), gcol),
        pl.BlockSpec((1, n), gcol),
        pl.BlockSpec((1, n), lambda g: (0, groups + g)),
        pl.BlockSpec((1, LANES), gcol),
        pl.BlockSpec((1, LANES), gcol),
    ]
    return pl.pallas_call(
        functools.partial(_ssm_sample_pre_body, cw=cw, head_dim=head_dim),
        grid=(groups,),
        in_specs=in_specs,
        out_specs=[pl.BlockSpec((nb, gw), gcol),
                   pl.BlockSpec((nb, gw), gcol),
                   pl.BlockSpec((nb, n), gcol),
                   pl.BlockSpec((nb, n), gcol),
                   pl.BlockSpec((nb, LANES), gcol)],
        out_shape=[jax.ShapeDtypeStruct((nb, d_ssm), F32),
                   jax.ShapeDtypeStruct((nb, d_ssm), BF16),
                   jax.ShapeDtypeStruct((nb, groups * n), F32),
                   jax.ShapeDtypeStruct((nb, groups * n), F32),
                   jax.ShapeDtypeStruct((nb, groups * LANES), F32)],
        compiler_params=_params(("parallel",), 32 * 1024 * 1024),
        name="ssm_sample_pre",
    )(xs, bc, bc, dt, hist, hist, hist, wx, wbc, wbc, bx, bbc, bbc, dtb, alog)


def _ssm_sample_state_body(decay_ref, xdt_t_ref, bg_ref, cg_ref, xs_ref, z_ref, dskip_ref, normg_ref, st_ref,
                           y_ref, nst_ref, *, groups, hg, head_dim):
    b = pl.program_id(0)
    nb = bg_ref.shape[1]
    n = bg_ref.shape[2]
    gw = hg * head_dim
    heads = groups * hg
    is_b = lax.broadcasted_iota(jnp.int32, (nb, n), 0) == b
    for g in range(groups):
        lo = g * gw
        bsel = jnp.where(is_b, bg_ref[g], 0.0).astype(BF16)
        outer = jnp.dot(xdt_t_ref[lo:lo + gw, :], bsel, preferred_element_type=F32)
        for h in range(hg):
            r0 = lo + h * head_dim
            d = decay_ref[b * heads + g * hg + h]
            nst_ref[0, r0:r0 + head_dim, :] = (st_ref[0, r0:r0 + head_dim, :] * d
                                               + outer[h * head_dim:(h + 1) * head_dim, :])
        c_row = jnp.broadcast_to(cg_ref[g, pl.ds(b, 1), :], (SUBLANES, n)).astype(BF16)
        y = lax.dot_general(c_row, nst_ref[0, lo:lo + gw, :].astype(BF16), (((1,), (1,)), ((), ())),
                            preferred_element_type=F32)[0:1, :]
        y = _group_rmsnorm_gate(y, xs_ref[pl.ds(b, 1), lo:lo + gw], z_ref[pl.ds(b, 1), lo:lo + gw],
                                dskip_ref[:, lo:lo + gw], normg_ref[:, lo:lo + gw])
        y_ref[0, :, lo:lo + gw] = y.astype(y_ref.dtype)


def _ssm_sample_state(decay, xdt_t, bg, cg, xs, z, dskip, normg, state, groups, head_dim):
    nb, d_ssm = xs.shape
    n = state.shape[-1]
    hg = d_ssm // groups // head_dim
    full2 = lambda b: (0, 0)
    full3 = lambda b: (0, 0, 0)
    return pl.pallas_call(
        functools.partial(_ssm_sample_state_body, groups=groups, hg=hg, head_dim=head_dim),
        grid=(nb,),
        in_specs=[pl.BlockSpec(memory_space=pltpu.SMEM),
                  pl.BlockSpec((d_ssm, nb), full2),
                  pl.BlockSpec((groups, nb, n), full3),
                  pl.BlockSpec((groups, nb, n), full3),
                  pl.BlockSpec((nb, d_ssm), full2),
                  pl.BlockSpec((nb, d_ssm), full2),
                  pl.BlockSpec((1, d_ssm), full2),
                  pl.BlockSpec((1, d_ssm), full2),
                  pl.BlockSpec((1, d_ssm, n), lambda b: (b, 0, 0))],
        out_specs=[pl.BlockSpec((1, 1, d_ssm), lambda b: (b, 0, 0)),
                   pl.BlockSpec((1, d_ssm, n), lambda b: (b, 0, 0))],
        out_shape=[jax.ShapeDtypeStruct((nb, 1, d_ssm), F32),
                   jax.ShapeDtypeStruct((nb, d_ssm, n), F32)],
        compiler_params=_params(("arbitrary",), 48 * 1024 * 1024),
        name="ssm_sample_state",
    )(decay, xdt_t, bg, cg, xs, z, dskip, normg, state)


def _softmax_rows(s):
    e = jnp.exp(s - jnp.max(s, axis=-1, keepdims=True))
    return e / jnp.sum(e, axis=-1, keepdims=True)


def _attn_prompt_body(q_ref, k_ref, v_ref, o_ref, *, scale):
    s = lax.dot_general(q_ref[...], k_ref[...].astype(BF16), (((1,), (1,)), ((), ())),
                        preferred_element_type=F32) * scale
    p = _softmax_rows(s)
    o_ref[...] = jnp.dot(p.astype(BF16), v_ref[...].astype(BF16), preferred_element_type=F32).astype(o_ref.dtype)


def _attn_prompt(q, k, v, batch, seqlen, n_mem, heads):
    m, d = q.shape
    hd = d // heads
    tq = min(512, seqlen)
    nq = seqlen // tq
    return pl.pallas_call(
        functools.partial(_attn_prompt_body, scale=hd ** -0.5),
        grid=(batch, heads, nq),
        in_specs=[pl.BlockSpec((tq, hd), lambda b, h, i: (b * nq + i, h)),
                  pl.BlockSpec((n_mem, hd), lambda b, h, i: (b, h)),
                  pl.BlockSpec((n_mem, hd), lambda b, h, i: (b, h))],
        out_specs=pl.BlockSpec((tq, hd), lambda b, h, i: (b * nq + i, h)),
        out_shape=jax.ShapeDtypeStruct((m, d), BF16),
        compiler_params=_params(("parallel", "parallel", "parallel"), 32 * 1024 * 1024),
        name="attn_prompt",
    )(q, k, v)


def _attn_sample_body(q_ref, k_ref, v_ref, o_ref, *, scale, chunk):
    q = q_ref[0]
    n_mem = k_ref.shape[2]
    s = jnp.concatenate(
        [jnp.sum(k_ref[0, 0, m0:m0 + chunk] * q[None], axis=-1, keepdims=True) for m0 in range(0, n_mem, chunk)],
        axis=0) * scale
    e = jnp.exp(s - jnp.max(s, axis=0, keepdims=True))
    p = e / jnp.sum(e, axis=0, keepdims=True)
    o = jnp.sum(p[0:chunk] * v_ref[0, 0, 0:chunk], axis=0)
    for m0 in range(chunk, n_mem, chunk):
        o = o + jnp.sum(p[m0:m0 + chunk] * v_ref[0, 0, m0:m0 + chunk], axis=0)
    o_ref[0] = o


def _attn_sample(q, k, v):
    _, nb, n_mem, heads, hd = k.shape
    chunk = min(32, n_mem)
    assert n_mem % chunk == 0
    kv_spec = pl.BlockSpec((1, 1, n_mem, heads, hd), lambda b: (0, b, 0, 0, 0))
    return pl.pallas_call(
        functools.partial(_attn_sample_body, scale=hd ** -0.5, chunk=chunk),
        grid=(nb,),
        in_specs=[pl.BlockSpec((1, heads, hd), lambda b: (b, 0, 0)), kv_spec, kv_spec],
        out_specs=pl.BlockSpec((1, heads, hd), lambda b: (b, 0, 0)),
        out_shape=jax.ShapeDtypeStruct((nb, heads, hd), F32),
        compiler_params=_params(("parallel",), 4 * n_mem * SUBLANES * hd * 4 + 8 * 1024 * 1024),
        name="attn_sample",
    )(q, k, v)


def kernel(x_prompt, x_sample, mem_prompt, state_conv, state_ssm_conv, state_ssm, cache_mem_k, cache_mem_v,
           g_mix, w_in, conv_w, conv_b, ln_g, ln_b, ssm_conv_w, ssm_conv_b, dt_bias, a_log, d_skip,
           ssm_norm_g, w_out, g_xattn, g_mem, w_q, w_k, w_v, w_o, g_mlp, w_up, w_down, g_final):
    depth = g_mix.shape[0]
    assert depth == 1
    batch, seqlen, d = x_prompt.shape
    nb = x_sample.shape[0]
    assert x_sample.shape[1] == 1
    d_conv = conv_w.shape[-1]
    c_xbc = ssm_conv_w.shape[-1]
    d_ssm = ssm_norm_g.shape[-1]
    heads = dt_bias.shape[-1]
    head_dim = d_ssm // heads
    n = state_ssm.shape[-1]
    groups = (c_xbc - d_ssm) // (2 * n)
    hg = heads // groups
    n_mem, xa_heads = cache_mem_k.shape[2], cache_mem_k.shape[3]
    cw = ssm_conv_w.shape[1]
    assert n == LANES and hg <= LANES

    o_z, o_x = 2 * d_conv, 2 * d_conv + d_ssm
    o_b, o_dt = o_x + d_ssm, o_x + c_xbc
    w_in_t = jnp.swapaxes(w_in, 1, 2)
    w_dt_t = jnp.pad(w_in_t[0, o_dt:].reshape(groups, hg, d), ((0, 0), (0, LANES - hg), (0, 0)))
    w_dt_t = w_dt_t.reshape(1, groups * LANES, d)
    wx, wbc = ssm_conv_w[0][:, :d_ssm], ssm_conv_w[0][:, d_ssm:]
    bx, bbc = ssm_conv_b[0][:d_ssm].reshape(1, d_ssm), ssm_conv_b[0][d_ssm:].reshape(1, c_xbc - d_ssm)
    dtb, alog = _pad_heads(dt_bias[0], groups), _pad_heads(a_log[0], groups)
    dskip = jnp.repeat(d_skip[0].astype(F32), head_dim).reshape(1, d_ssm)
    normg = ssm_norm_g[0].reshape(1, d_ssm)

    mp = batch * seqlen
    xp, xsm = x_prompt.reshape(mp, d), x_sample.reshape(nb, d)
    norm2 = lambda a_p, a_s, g, dtype: (_rmsnorm(a_p, g, dtype), _rmsnorm(a_s, g, dtype))
    in_proj = functools.partial(_proj, *norm2(xp, xsm, g_mix[0], BF16), out_dtype=F32, w_rows_are_outputs=True)
    glu, glu_s = in_proj(w_in_t, col0=0, ncols=o_z)
    z, z_s = in_proj(w_in_t, col0=o_z, ncols=d_ssm)
    xs, xs_s = in_proj(w_in_t, col0=o_x, ncols=d_ssm)
    bc, bc_s = in_proj(w_in_t, col0=o_b, ncols=o_dt - o_b)
    dt, dt_s = in_proj(w_dt_t)

    v_p, conv_hist = _conv_prompt(glu, conv_w[0], conv_b[0], ln_g[0], ln_b[0], batch, seqlen)
    y_p, st_p, w_out16 = _ssd_prompt(xs, z, bc, dt, wx, wbc, bx, bbc, dtb, alog, dskip, normg,
                                     batch, seqlen, groups, head_dim, n, cast_riders=(w_out[0],))
    keep = conv_w.shape[1] - 1
    new_conv_p = conv_hist[:, CONV_HALO - keep:, :]
    tail = lambda a: a.reshape(batch, seqlen, -1)[:, seqlen - (cw - 1):, :]
    new_ssm_conv_p = jnp.concatenate([tail(xs), tail(bc)], axis=-1)

    v_s, new_conv_t = _conv_sample(glu_s, jnp.swapaxes(state_conv[0], 0, 1), conv_w[0], conv_b[0], ln_g[0], ln_b[0])
    new_conv_s = jnp.swapaxes(new_conv_t, 0, 1)[None]
    hist = jnp.swapaxes(state_ssm_conv[0], 0, 1)
    xs_c, xdt, b_c, c_c, decay = _ssm_sample_pre(xs_s, bc_s, dt_s, hist, wx, wbc, bx, bbc, dtb, alog,
                                                  groups, head_dim, n)
    decay_flat = decay.reshape(nb, groups, LANES)[:, :, :hg].reshape(nb * heads)
    to_groups = lambda a: jnp.swapaxes(a.reshape(nb, groups, n), 0, 1)
    y_s, st_s = _ssm_sample_state(decay_flat, xdt.T, to_groups(b_c), to_groups(c_c), xs_c, z_s, dskip, normg,
                                  state_ssm[0].reshape(nb, d_ssm, n), groups, head_dim)
    new_ssm_conv_s = jnp.swapaxes(
        jnp.concatenate([hist[1:], jnp.concatenate([xs_s, bc_s], axis=-1)[None]], axis=0), 0, 1)

    x1, x1_s = _kgrid([v_p, y_p], [v_s, y_s.reshape(nb, d_ssm).astype(BF16)], w_out16,
                      res_p=xp, res_s=xsm)
    q, q_s = _proj(*norm2(x1, x1_s, g_xattn[0], BF16), w_q, out_dtype=BF16)
    mem_n = _rmsnorm(mem_prompt.reshape(batch * n_mem, d), g_mem[0], BF16)
    k_p, _ = _proj(mem_n, None, w_k, out_dtype=F32)
    v_mem_p, _ = _proj(mem_n, None, w_v, out_dtype=F32)
    o = _attn_prompt(q, k_p, v_mem_p, batch, seqlen, n_mem, xa_heads)
    o_s = _attn_sample(q_s.astype(F32).reshape(nb, xa_heads, d // xa_heads), cache_mem_k,
                       cache_mem_v).reshape(nb, d).astype(BF16)
    x2, x2_s = _proj(o, o_s, w_o, out_dtype=F32, res_p=x1, res_s=x1_s)
    up, up_s, w_down16 = _proj(*norm2(x2, x2_s, g_mlp[0], BF16), w_up, out_dtype=BF16, act="relu2",
                               cast_rider=w_down[0])
    x3, x3_s = _kgrid([up], [up_s], w_down16, res_p=x2, res_s=x2_s)
    y_prompt, y_sample = norm2(x3, x3_s, g_final, F32)

    kv_shape = (1, batch, n_mem, xa_heads, d // xa_heads)
    return (y_prompt.reshape(batch, seqlen, d), y_sample.reshape(nb, 1, d),
            new_conv_p[None], new_ssm_conv_p[None], st_p.reshape(1, batch, heads, head_dim, n),
            k_p.reshape(kv_shape), v_mem_p.reshape(kv_shape),
            new_conv_s, new_ssm_conv_s[None], st_s.reshape(1, nb, heads, head_dim, n))
```

```python
import functools

import jax
import jax.numpy as jnp
from jax import lax
from jax.experimental import pallas as pl
from jax.experimental.pallas import tpu as pltpu

F32 = jnp.float32
BF16 = jnp.bfloat16
EPS = 1e-5
LANES = 128
SUBLANES = 8
MXU_DIM = 256
VMEM_CAP = 62 * 1024 * 1024
PROJ_VMEM_SLACK = 4 * 1024 * 1024


def _params(sem, vmem_bytes):
    return pltpu.CompilerParams(dimension_semantics=sem,
                                vmem_limit_bytes=int(min(VMEM_CAP, max(vmem_bytes, 16 * 1024 * 1024))))


def _sigmoid(x):
    return 0.5 * jnp.tanh(0.5 * x) + 0.5


def _silu(x):
    h = 0.5 * x
    return h * jnp.tanh(h) + h


def _split3(x):
    hi = x.astype(BF16)
    r1 = x - hi.astype(F32)
    mid = r1.astype(BF16)
    return hi, mid, (r1 - mid.astype(F32)).astype(BF16)


def _tile(dim, cap, unit):
    if dim <= cap:
        return dim
    best = max(t for t in range(unit, cap + 1, unit) if dim % t == 0)
    return best


def _rmsnorm_body(x_ref, g_ref, o_ref):
    x = x_ref[...]
    ms = jnp.mean(x * x, axis=-1, keepdims=True)
    o_ref[...] = (x * lax.rsqrt(ms + EPS) * g_ref[...]).astype(o_ref.dtype)


def _rmsnorm(x, g, out_dtype, bm=256):
    m, d = x.shape
    bm = min(bm, m)
    return pl.pallas_call(
        _rmsnorm_body,
        grid=(m // bm,),
        in_specs=[pl.BlockSpec((bm, d), lambda i: (i, 0)),
                  pl.BlockSpec((1, d), lambda i: (0, 0))],
        out_specs=pl.BlockSpec((bm, d), lambda i: (i, 0)),
        out_shape=jax.ShapeDtypeStruct((m, d), out_dtype),
        compiler_params=_params(("parallel",), 6 * bm * d * 4),
        name="rmsnorm",
    )(x, g.reshape(1, d).astype(F32))


def _epilogue(acc, res_ref, o_ref, act):
    if act == "relu2":
        acc = jnp.square(jnp.maximum(acc, 0.0))
    if res_ref is not None:
        acc = res_ref[...] + acc
    o_ref[...] = acc.astype(o_ref.dtype)


def _proj_body(*refs, act, has_s, has_res, has_rider, w_rows_are_outputs, n_panels):
    if w_rows_are_outputs:
        mm = lambda a, w: lax.dot_general(a, w, (((1,), (1,)), ((), ())), preferred_element_type=F32)
    else:
        mm = lambda a, w: jnp.dot(a, w, preferred_element_type=F32)
    it = iter(refs)
    ap_ref = next(it)
    as_ref = next(it) if has_s else None
    w_ref = next(it)
    rp_ref = next(it) if has_res else None
    rs_ref = next(it) if has_res and has_s else None
    rider_ref = next(it) if has_rider else None
    op_ref = next(it)
    os_ref = next(it) if has_s else None
    rider_out_ref = next(it) if has_rider else None
    w16 = next(it)
    p, i = pl.program_id(0), pl.program_id(1)
    chunk = w_ref.shape[0]
    if has_rider:
        rider_out_ref[...] = rider_ref[...].astype(rider_out_ref.dtype)

    @pl.when(p < n_panels)
    def _():
        w16[p % 2, pl.ds(pl.multiple_of(i * chunk, chunk), chunk), :] = w_ref[...].astype(BF16)

    @pl.when(p >= 1)
    def _():
        panel = w16.at[(p - 1) % 2]
        if has_s:
            @pl.when(i == 0)
            def _():
                _epilogue(mm(as_ref[...], panel[...]), rs_ref, os_ref, act)
        _epilogue(mm(ap_ref[...], panel[...]), rp_ref, op_ref, act)


def _proj(a_p, a_s, w, *, out_dtype, col0=0, ncols=None, act=None, res_p=None, res_s=None, w_rows_are_outputs=False,
          cast_rider=None):
    mp, kdim = a_p.shape
    ncols = w.shape[1 if w_rows_are_outputs else 2] - col0 if ncols is None else ncols
    has_s, has_res = a_s is not None, res_p is not None
    osz = jnp.dtype(out_dtype).itemsize
    ms = a_s.shape[0] if has_s else 0

    def footprint(bm, bn):
        return (2 * kdim * bn * 2 + 2 * kdim * bn * 4 // (mp // bm) + 2 * (bm + ms) * kdim * 2
                + (2 * osz + 2 * 4 * has_res + 4) * (bm + ms) * bn)

    bm = _tile(mp, 1024, SUBLANES)
    ni = mp // bm
    for bn_cap in (1024, 512, 256):
        bn = max(t for t in range(LANES, bn_cap + 1, LANES) if col0 % t == 0 and ncols % t == 0)
        if footprint(bm, bn) + PROJ_VMEM_SLACK <= VMEM_CAP:
            break
    c0, nj = col0 // bn, ncols // bn
    vmem = footprint(bm, bn) + PROJ_VMEM_SLACK
    row = lambda p, i: jnp.where(p == 0, 0, i)
    col = lambda p: jnp.maximum(p - 1, 0)
    nxt = lambda p: jnp.minimum(p, nj - 1)
    in_specs = [pl.BlockSpec((bm, kdim), lambda p, i: (row(p, i), 0))]
    args = [a_p]
    out_specs = [pl.BlockSpec((bm, bn), lambda p, i: (row(p, i), col(p)))]
    out_shape = [jax.ShapeDtypeStruct((mp, ncols), out_dtype)]
    if has_s:
        in_specs.append(pl.BlockSpec((ms, kdim), lambda p, i: (0, 0)))
        args.append(a_s)
        out_specs.append(pl.BlockSpec((ms, bn), lambda p, i: (0, col(p))))
        out_shape.append(jax.ShapeDtypeStruct((ms, ncols), out_dtype))
    if w_rows_are_outputs:
        chunk = bn // ni
        in_specs.append(pl.BlockSpec((None, chunk, kdim), lambda p, i: (0, (c0 + nxt(p)) * ni + i, 0)))
    else:
        chunk = kdim // ni
        in_specs.append(pl.BlockSpec((None, chunk, bn), lambda p, i: (0, i, c0 + nxt(p))))
    assert chunk * ni == (bn if w_rows_are_outputs else kdim) and chunk % (2 * SUBLANES) == 0
    args.append(w)
    if has_res:
        in_specs.append(pl.BlockSpec((bm, bn), lambda p, i: (row(p, i), col(p))))
        args.append(res_p)
        if has_s:
            in_specs.append(pl.BlockSpec((ms, bn), lambda p, i: (0, col(p))))
            args.append(res_s)
    if cast_rider is not None:
        slab = cast_rider.shape[0] // (nj * ni)
        assert slab * nj * ni == cast_rider.shape[0] and slab % (2 * SUBLANES) == 0
        rider_spec = pl.BlockSpec((slab, cast_rider.shape[1]), lambda p, i: (col(p) * ni + row(p, i), 0))
        in_specs.append(rider_spec)
        args.append(cast_rider)
        out_specs.append(rider_spec)
        out_shape.append(jax.ShapeDtypeStruct(cast_rider.shape, BF16))
        vmem += 2 * slab * cast_rider.shape[1] * (4 + 2)
    out = pl.pallas_call(
        functools.partial(_proj_body, act=act, has_s=has_s, has_res=has_res, has_rider=cast_rider is not None,
                          w_rows_are_outputs=w_rows_are_outputs, n_panels=nj),
        grid=(nj + 1, ni),
        in_specs=in_specs,
        out_specs=out_specs,
        out_shape=out_shape,
        scratch_shapes=[pltpu.VMEM((2, bn, kdim) if w_rows_are_outputs else (2, kdim, bn), BF16)],
        compiler_params=_params(("arbitrary", "arbitrary"), vmem),
        name="proj",
    )(*args)
    main = (out[0], out[1]) if has_s else (out[0], None)
    return main if cast_rider is None else (*main, out[-1])


def _kgrid_body(*refs, part_blocks, has_res):
    nparts = len(part_blocks)
    nk = sum(part_blocks)
    it = iter(refs)
    ap_refs = [next(it) for _ in range(nparts)]
    as_refs = [next(it) for _ in range(nparts)]
    w_ref = next(it)
    rp_ref = next(it) if has_res else None
    rs_ref = next(it) if has_res else None
    op_ref, os_ref, accp_ref, accs_ref = next(it), next(it), next(it), next(it)
    i, k = pl.program_id(1), pl.program_id(2)

    def reduce_k(a_refs, acc_ref, res_ref, o_ref):
        lo = 0
        for p, blocks in enumerate(part_blocks):
            hi = lo + blocks
            first, last = max(lo, 1), min(hi, nk - 1)
            dot = lambda p=p: jnp.dot(a_refs[p][...], w_ref[...], preferred_element_type=F32)
            if lo == 0:
                @pl.when(k == 0)
                def _(dot=dot):
                    acc_ref[...] = dot()
            if last > first:
                @pl.when((k >= first) & (k < last))
                def _(dot=dot):
                    acc_ref[...] += dot()
            if hi == nk:
                @pl.when(k == nk - 1)
                def _(dot=dot):
                    _epilogue(acc_ref[...] + dot(), res_ref, o_ref, None)
            lo = hi

    reduce_k(ap_refs, accp_ref, rp_ref, op_ref)

    @pl.when(i == 0)
    def _():
        reduce_k(as_refs, accs_ref, rs_ref, os_ref)


def _kgrid(parts_p, parts_s, w, *, res_p=None, res_s=None):
    mp, ms = parts_p[0].shape[0], parts_s[0].shape[0]
    kdim, n = w.shape
    assert sum(a.shape[1] for a in parts_p) == kdim
    bm = _tile(mp, 1024, SUBLANES)
    bn = _tile(n, 1024, LANES)
    has_res = res_p is not None
    footprint = lambda bk: (2 * len(parts_p) * (bm + ms) * bk * 2 + 2 * bk * bn * 2
                            + (4 + 2 * has_res) * (bm + ms) * bn * 4)
    bk = 4096
    while any(a.shape[1] % bk for a in parts_p) or kdim // bk < 2 or footprint(bk) + PROJ_VMEM_SLACK > VMEM_CAP:
        bk //= 2
    part_blocks = tuple(a.shape[1] // bk for a in parts_p)
    nk = sum(part_blocks)
    assert nk >= 2 and bk % LANES == 0
    vmem = footprint(bk) + PROJ_VMEM_SLACK
    lhs_specs = lambda rows, row_of: [
        pl.BlockSpec((rows, bk), lambda j, i, k, lo=lo, blocks=blocks: (row_of(i), jnp.clip(k - lo, 0, blocks - 1)))
        for lo, blocks in zip([sum(part_blocks[:p]) for p in range(len(part_blocks))], part_blocks)]
    in_specs = lhs_specs(bm, lambda i: i) + lhs_specs(ms, lambda i: 0)
    in_specs.append(pl.BlockSpec((bk, bn), lambda j, i, k: (k, j)))
    args = [*parts_p, *parts_s, w]
    if has_res:
        in_specs += [pl.BlockSpec((bm, bn), lambda j, i, k: (i, j)), pl.BlockSpec((ms, bn), lambda j, i, k: (0, j))]
        args += [res_p, res_s]
    return pl.pallas_call(
        functools.partial(_kgrid_body, part_blocks=part_blocks, has_res=has_res),
        grid=(n // bn, mp // bm, nk),
        in_specs=in_specs,
        out_specs=[pl.BlockSpec((bm, bn), lambda j, i, k: (i, j)), pl.BlockSpec((ms, bn), lambda j, i, k: (0, j))],
        out_shape=[jax.ShapeDtypeStruct((mp, n), F32), jax.ShapeDtypeStruct((ms, n), F32)],
        scratch_shapes=[pltpu.VMEM((bm, bn), F32), pltpu.VMEM((ms, bn), F32)],
        compiler_params=_params(("parallel", "arbitrary", "arbitrary"), vmem),
        name="kgrid",
    )(*args)


CONV_HALO = 32
CONV_ROWS = 32
CONV_STRIP = 512


def _conv_prompt_body(a_ref, g_ref, w_ref, b_ref, lng_ref, lnb_ref, v_ref, hist_ref, ubuf, sbuf, cbuf,
                      *, tile, width, nt):
    t = pl.program_id(1)
    ch = ubuf.shape[1]

    @pl.when(t == 0)
    def _():
        ubuf[0:CONV_HALO, :] = jnp.zeros((CONV_HALO, ch), F32)

    ubuf[CONV_HALO:CONV_HALO + tile, :] = a_ref[...] * _sigmoid(g_ref[...])
    off0 = CONV_HALO - (width - 1)
    srows = CONV_HALO + tile - SUBLANES

    def strip(c, carry):
        col = pl.ds(pl.multiple_of(c * CONV_STRIP, CONV_STRIP), CONV_STRIP)
        for r in range(1, SUBLANES):
            sbuf[r, 0:srows, :] = ubuf[pl.ds(r, srows), col]
        for r0 in range(0, tile, CONV_ROWS):
            acc = jnp.broadcast_to(b_ref[:, col], (CONV_ROWS, CONV_STRIP))
            for k in range(width):
                whole, r = divmod(off0 + k, SUBLANES)
                if r == 0:
                    rows = ubuf[pl.ds(r0 + off0 + k, CONV_ROWS), col]
                else:
                    rows = sbuf[r, pl.ds(r0 + whole * SUBLANES, CONV_ROWS), :]
                acc = acc + w_ref[k:k + 1, col] * rows
            cbuf[r0:r0 + CONV_ROWS, col] = acc
        return carry

    lax.fori_loop(0, ch // CONV_STRIP, strip, 0)

    y = cbuf[...]
    yc = y - jnp.mean(y, axis=-1, keepdims=True)
    yn = yc * lax.rsqrt(jnp.mean(yc * yc, axis=-1, keepdims=True) + EPS)
    v_ref[...] = _silu(yn * lng_ref[...] + lnb_ref[...]).astype(v_ref.dtype)

    ubuf[0:CONV_HALO, :] = ubuf[tile:tile + CONV_HALO, :]

    @pl.when(t == nt - 1)
    def _():
        hist_ref[0] = ubuf[0:CONV_HALO, :]


def _conv_prompt(glu, conv_w, conv_b, ln_g, ln_b, batch, seqlen):
    width, ch = conv_w.shape
    tile = min(128, seqlen)
    nt = seqlen // tile
    assert width - 1 <= CONV_HALO <= tile and seqlen % tile == 0
    assert ch % CONV_STRIP == 0 and tile % CONV_ROWS == 0
    row = lambda b, t: (b * nt + t, 0)
    vec = pl.BlockSpec((1, ch), lambda b, t: (0, 0))
    return pl.pallas_call(
        functools.partial(_conv_prompt_body, tile=tile, width=width, nt=nt),
        grid=(batch, nt),
        in_specs=[pl.BlockSpec((tile, ch), row),
                  pl.BlockSpec((tile, ch), lambda b, t: (b * nt + t, 1)),
                  pl.BlockSpec((width, ch), lambda b, t: (0, 0)),
                  vec, vec, vec],
        out_specs=[pl.BlockSpec((tile, ch), row),
                   pl.BlockSpec((1, CONV_HALO, ch), lambda b, t: (b, 0, 0))],
        out_shape=[jax.ShapeDtypeStruct((batch * seqlen, ch), BF16),
                   jax.ShapeDtypeStruct((batch, CONV_HALO, ch), F32)],
        scratch_shapes=[pltpu.VMEM((CONV_HALO + tile, ch), F32),
                        pltpu.VMEM((SUBLANES, CONV_HALO + tile, CONV_STRIP), F32),
                        pltpu.VMEM((tile, ch), F32)],
        compiler_params=_params(("arbitrary", "arbitrary"), 32 * 1024 * 1024),
        name="conv_prompt",
    )(glu, glu, conv_w, conv_b.reshape(1, ch), ln_g.reshape(1, ch), ln_b.reshape(1, ch))


def _conv_sample_body(a_ref, g_ref, st_ref, w_ref, b_ref, lng_ref, lnb_ref, v_ref, nst_ref, *, width):
    u = a_ref[...] * _sigmoid(g_ref[...])
    y = b_ref[...] + w_ref[width - 1:width, :] * u
    for k in range(width - 1):
        y = y + w_ref[k:k + 1, :] * st_ref[k]
    yc = y - jnp.mean(y, axis=-1, keepdims=True)
    yn = yc * lax.rsqrt(jnp.mean(yc * yc, axis=-1, keepdims=True) + EPS)
    v_ref[...] = _silu(yn * lng_ref[...] + lnb_ref[...]).astype(v_ref.dtype)
    for k in range(width - 2):
        nst_ref[k] = st_ref[k + 1]
    nst_ref[width - 2] = u


def _conv_sample(glu, state, conv_w, conv_b, ln_g, ln_b):
    width, ch = conv_w.shape
    nb = glu.shape[0]
    bs = _tile(nb, 2 * SUBLANES, SUBLANES)
    assert nb % bs == 0
    vec = pl.BlockSpec((1, ch), lambda i: (0, 0))
    return pl.pallas_call(
        functools.partial(_conv_sample_body, width=width),
        grid=(nb // bs,),
        in_specs=[pl.BlockSpec((bs, ch), lambda i: (i, 0)),
                  pl.BlockSpec((bs, ch), lambda i: (i, 1)),
                  pl.BlockSpec((width - 1, bs, ch), lambda i: (0, i, 0)),
                  pl.BlockSpec((width, ch), lambda i: (0, 0)),
                  vec, vec, vec],
        out_specs=[pl.BlockSpec((bs, ch), lambda i: (i, 0)),
                   pl.BlockSpec((width - 1, bs, ch), lambda i: (0, i, 0))],
        out_shape=[jax.ShapeDtypeStruct((nb, ch), BF16),
                   jax.ShapeDtypeStruct(state.shape, F32)],
        compiler_params=_params(("parallel",), 32 * 1024 * 1024),
        name="conv_sample",
    )(glu, glu, state, conv_w, conv_b.reshape(1, ch), ln_g.reshape(1, ch), ln_b.reshape(1, ch))


def _expand_heads(v, heads, head_dim):
    rows = v.shape[0]
    per_tile = LANES // head_dim
    lane = lax.broadcasted_iota(jnp.int32, (rows, LANES), 1)
    tiles = []
    for j in range(heads // per_tile):
        h0 = j * per_tile
        t = jnp.broadcast_to(v[:, h0:h0 + 1], (rows, LANES))
        for i in range(1, per_tile):
            t = jnp.where(lane >= i * head_dim, jnp.broadcast_to(v[:, h0 + i:h0 + i + 1], (rows, LANES)), t)
        tiles.append(t)
    return jnp.concatenate(tiles, axis=1)


def _pad_heads(v, groups):
    hg = v.shape[0] // groups
    return jnp.pad(v.reshape(groups, hg).astype(F32), ((0, 0), (0, LANES - hg))).reshape(1, groups * LANES)


def _group_rmsnorm_gate(y, xs, z, dskip, normg):
    y = (y + dskip * xs) * _silu(z)
    return y * lax.rsqrt(jnp.mean(y * y, axis=-1, keepdims=True) + EPS) * normg


def _ssd_prompt_body(xs_ref, z_ref, b_ref, c_ref, dt_ref, wx_ref, wb_ref, wc_ref, bx_ref, bb_ref, bc_ref,
                     dtb_ref, alog_ref, dskip_ref, normg_ref, bd_ref, rep_ref,
                     y_ref, st_ref, state, xbuf, bbuf, cbuf, *, q, nc, head_dim, cw):
    c = pl.program_id(2)
    gw = xbuf.shape[1]
    n = bbuf.shape[1]
    hg = gw // head_dim
    halo = SUBLANES

    @pl.when(c == 0)
    def _():
        state[...] = jnp.zeros(state.shape, F32)
        xbuf[0:halo, :] = jnp.zeros((halo, gw), F32)
        bbuf[0:halo, :] = jnp.zeros((halo, n), F32)
        cbuf[0:halo, :] = jnp.zeros((halo, n), F32)

    def causal_conv(raw_ref, buf, w_ref, bias_ref):
        buf[halo:halo + q, :] = raw_ref[...]
        rows = buf[...]
        acc = bias_ref[...] + w_ref[cw - 1:cw, :] * rows[halo:]
        for j in range(1, cw):
            acc = acc + w_ref[cw - 1 - j:cw - j, :] * pltpu.roll(rows, j, axis=0)[halo:]
        buf[0:halo, :] = rows[q:q + halo]
        return _silu(acc)

    xs = causal_conv(xs_ref, xbuf, wx_ref, bx_ref)
    bm = causal_conv(b_ref, bbuf, wb_ref, bb_ref)
    cm = causal_conv(c_ref, cbuf, wc_ref, bc_ref)

    dt = jax.nn.softplus(dt_ref[...] + dtb_ref[...])
    dta = dt * (-jnp.exp(alog_ref[...]))
    qi = lax.broadcasted_iota(jnp.int32, (q, q), 0)
    si = lax.broadcasted_iota(jnp.int32, (q, q), 1)
    causal = qi >= si
    tri = causal.astype(BF16)
    hi, mid, lo = _split3(dta)
    cs = (jnp.dot(tri, hi, preferred_element_type=F32) + jnp.dot(tri, mid, preferred_element_type=F32)
          + jnp.dot(tri, lo, preferred_element_type=F32))
    cs_t = cs.T
    dt_t = dt.T
    per_head = jnp.concatenate([jnp.exp(cs), dt * jnp.exp(cs[q - 1:q, :] - cs)], axis=0)
    spread = rep_ref[...]
    per_lane = sum(jnp.dot(piece, spread, preferred_element_type=F32) for piece in _split3(per_head))
    exp_cs_x = per_lane[0:q]
    xdt_end = xs * per_lane[q:2 * q]

    xs16 = xs.astype(BF16)
    cm16 = cm.astype(BF16)
    cb = lax.dot_general(cm16, bm.astype(BF16), (((1,), (1,)), ((), ())), preferred_element_type=F32)

    hb = MXU_DIM // head_dim
    slab = hb * head_dim
    cb = jnp.where(causal, cb, 0.0)
    blockdiag = bd_ref[...]
    y_parts = []
    for s in range(hg // hb):
        lhs = []
        for j in range(hb):
            h = s * hb + j
            seg = cs[:, h:h + 1] - cs_t[h:h + 1, :]
            lmat = jnp.exp(jnp.where(causal, seg, 0.0))
            lhs.append((cb * lmat * dt_t[h:h + 1, :]).astype(BF16))
        xslab = xs16[:, s * slab:(s + 1) * slab]
        rhs = jnp.concatenate([xslab] * hb, axis=0) * blockdiag
        y_parts.append(jnp.dot(jnp.concatenate(lhs, axis=1), rhs, preferred_element_type=F32))
    y = jnp.concatenate(y_parts, axis=1)

    st = state[...]
    y = y + jnp.dot(cm16, st.astype(BF16), preferred_element_type=F32) * exp_cs_x
    upd = jnp.dot(bm.T.astype(BF16), xdt_end.astype(BF16), preferred_element_type=F32)
    new_state = st * exp_cs_x[q - 1:q, :] + upd
    state[...] = new_state

    y_ref[...] = _group_rmsnorm_gate(y, xs, z_ref[...], dskip_ref[...], normg_ref[...]).astype(y_ref.dtype)

    @pl.when(c == nc - 1)
    def _():
        st_ref[0] = new_state.T


N_SSD_INPUTS = 17


def _ssd_prompt_and_casts_body(*refs, n_cast, **kw):
    ins, riders = refs[:N_SSD_INPUTS], refs[N_SSD_INPUTS:N_SSD_INPUTS + n_cast]
    outs = refs[N_SSD_INPUTS + n_cast:N_SSD_INPUTS + n_cast + 2]
    rider_outs = refs[N_SSD_INPUTS + n_cast + 2:N_SSD_INPUTS + 2 * n_cast + 2]
    scratch = refs[N_SSD_INPUTS + 2 * n_cast + 2:]
    for src, dst in zip(riders, rider_outs):
        dst[...] = src[...].astype(dst.dtype)
    _ssd_prompt_body(*ins, *outs, *scratch, **kw)


def _ssd_prompt(xs, z, bc, dt, wx, wbc, bx, bbc, dtb, alog, dskip, normg, batch, seqlen, groups, head_dim, n,
                cast_riders=()):
    m, d_ssm = xs.shape
    gw = d_ssm // groups
    hg = gw // head_dim
    cw = wx.shape[0]
    q = min(128, seqlen)
    nc = seqlen // q
    hb = MXU_DIM // head_dim
    assert seqlen % q == 0 and q % LANES == 0 and n == LANES and hg <= LANES
    assert gw % MXU_DIM == 0 and MXU_DIM % head_dim == 0 and cw - 1 <= SUBLANES
    row = lambda b, g, c: (b * nc + c, g)
    gcol = lambda b, g, c: (0, g)
    in_specs = [
        pl.BlockSpec((q, gw), row),
        pl.BlockSpec((q, gw), row),
        pl.BlockSpec((q, n), row),
        pl.BlockSpec((q, n), lambda b, g, c: (b * nc + c, groups + g)),
        pl.BlockSpec((q, LANES), row),
        pl.BlockSpec((cw, gw), gcol),
        pl.BlockSpec((cw, n), gcol),
        pl.BlockSpec((cw, n), lambda b, g, c: (0, groups + g)),
        pl.BlockSpec((1, gw), gcol),
        pl.BlockSpec((1, n), gcol),
        pl.BlockSpec((1, n), lambda b, g, c: (0, groups + g)),
        pl.BlockSpec((1, LANES), gcol),
        pl.BlockSpec((1, LANES), gcol),
        pl.BlockSpec((1, gw), gcol),
        pl.BlockSpec((1, gw), gcol),
        pl.BlockSpec((hb * q, MXU_DIM), lambda b, g, c: (0, 0)),
        pl.BlockSpec((LANES, gw), lambda b, g, c: (0, 0)),
    ]
    spread = (jnp.arange(LANES)[:, None] == (jnp.arange(gw)[None, :] // head_dim)).astype(BF16)
    blockdiag = ((jnp.arange(hb * q)[:, None] // q) == (jnp.arange(MXU_DIM)[None, :] // head_dim)).astype(BF16)
    assert len(in_specs) == N_SSD_INPUTS
    steps = batch * groups * nc
    rider_specs = []
    for r in cast_riders:
        slab = r.shape[0] // steps
        assert r.shape[0] % steps == 0 and slab % (2 * SUBLANES) == 0
        rider_specs.append(pl.BlockSpec((slab, r.shape[1]), lambda b, g, c: ((b * groups + g) * nc + c, 0)))
    return pl.pallas_call(
        functools.partial(_ssd_prompt_and_casts_body, n_cast=len(cast_riders), q=q, nc=nc, head_dim=head_dim, cw=cw),
        grid=(batch, groups, nc),
        in_specs=in_specs + rider_specs,
        out_specs=[pl.BlockSpec((q, gw), row),
                   pl.BlockSpec((1, gw, n), lambda b, g, c: (b, g, 0))] + rider_specs,
        out_shape=[jax.ShapeDtypeStruct((m, d_ssm), BF16),
                   jax.ShapeDtypeStruct((batch, d_ssm, n), F32)]
                  + [jax.ShapeDtypeStruct(r.shape, BF16) for r in cast_riders],
        scratch_shapes=[pltpu.VMEM((n, gw), F32),
                        pltpu.VMEM((SUBLANES + q, gw), F32),
                        pltpu.VMEM((SUBLANES + q, n), F32),
                        pltpu.VMEM((SUBLANES + q, n), F32)],
        compiler_params=_params(("arbitrary", "arbitrary", "arbitrary"), 32 * 1024 * 1024),
        name="ssd_prompt",
    )(xs, z, bc, bc, dt, wx, wbc, wbc, bx, bbc, bbc, dtb, alog, dskip, normg, blockdiag, spread, *cast_riders)


def _ssm_sample_pre_body(xs_ref, b_ref, c_ref, dt_ref, hx_ref, hb_ref, hc_ref, wx_ref, wb_ref, wc_ref,
                         bx_ref, bb_ref, bc_ref, dtb_ref, alog_ref,
                         xs_o, xdt_o, b_o, c_o, decay_o, *, cw, head_dim):
    def conv(new_ref, hist_ref, w_ref, bias_ref):
        acc = bias_ref[...] + w_ref[cw - 1:cw, :] * new_ref[...]
        for k in range(cw - 1):
            acc = acc + w_ref[k:k + 1, :] * hist_ref[k]
        return _silu(acc)

    xs = conv(xs_ref, hx_ref, wx_ref, bx_ref)
    b_o[...] = conv(b_ref, hb_ref, wb_ref, bb_ref)
    c_o[...] = conv(c_ref, hc_ref, wc_ref, bc_ref)
    dt = jax.nn.softplus(dt_ref[...] + dtb_ref[...])
    decay_o[...] = jnp.exp(dt * (-jnp.exp(alog_ref[...])))
    xs_o[...] = xs
    xdt_o[...] = (xs * _expand_heads(dt, xs.shape[1] // head_dim, head_dim)).astype(xdt_o.dtype)


def _ssm_sample_pre(xs, bc, dt, hist, wx, wbc, bx, bbc, dtb, alog, groups, head_dim, n):
    nb, d_ssm = xs.shape
    gw = d_ssm // groups
    hg = gw // head_dim
    cw = wx.shape[0]
    xoff = d_ssm // n
    gcol = lambda g: (0, g)
    in_specs = [
        pl.BlockSpec((nb, gw), gcol),
        pl.BlockSpec((nb, n), gcol),
        pl.BlockSpec((nb, n), lambda g: (0, groups + g)),
        pl.BlockSpec((nb, LANES), gcol),
        pl.BlockSpec((cw - 1, nb, gw), lambda g: (0, 0, g)),
        pl.BlockSpec((cw - 1, nb, n), lambda g: (0, 0, xoff + g)),
        pl.BlockSpec((cw - 1, nb, n), lambda g: (0, 0, xoff + groups + g)),
        pl.BlockSpec((cw, gw), gcol),
        pl.BlockSpec((cw, n), gcol),
        pl.BlockSpec((cw, n), lambda g: (0, groups + g)),
        pl.BlockSpec((1, gw), gcol),
        pl.BlockSpec((1, n), gcol),
        pl.BlockSpec((1, n), lambda g: (0, groups + g)),
        pl.BlockSpec((1, LANES), gcol),
        pl.BlockSpec((1, LANES), gcol),
    ]
    return pl.pallas_call(
        functools.partial(_ssm_sample_pre_body, cw=cw, head_dim=head_dim),
        grid=(groups,),
        in_specs=in_specs,
        out_specs=[pl.BlockSpec((nb, gw), gcol),
                   pl.BlockSpec((nb, gw), gcol),
                   pl.BlockSpec((nb, n), gcol),
                   pl.BlockSpec((nb, n), gcol),
                   pl.BlockSpec((nb, LANES), gcol)],
        out_shape=[jax.ShapeDtypeStruct((nb, d_ssm), F32),
                   jax.ShapeDtypeStruct((nb, d_ssm), BF16),
                   jax.ShapeDtypeStruct((nb, groups * n), F32),
                   jax.ShapeDtypeStruct((nb, groups * n), F32),
                   jax.ShapeDtypeStruct((nb, groups * LANES), F32)],
        compiler_params=_params(("parallel",), 32 * 1024 * 1024),
        name="ssm_sample_pre",
    )(xs, bc, bc, dt, hist, hist, hist, wx, wbc, wbc, bx, bbc, bbc, dtb, alog)


def _ssm_sample_state_body(decay_ref, xdt_t_ref, bg_ref, cg_ref, xs_ref, z_ref, dskip_ref, normg_ref, st_ref,
                           y_ref, nst_ref, *, groups, hg, head_dim):
    b = pl.program_id(0)
    nb = bg_ref.shape[1]
    n = bg_ref.shape[2]
    gw = hg * head_dim
    heads = groups * hg
    is_b = lax.broadcasted_iota(jnp.int32, (nb, n), 0) == b
    for g in range(groups):
        lo = g * gw
        bsel = jnp.where(is_b, bg_ref[g], 0.0).astype(BF16)
        outer = jnp.dot(xdt_t_ref[lo:lo + gw, :], bsel, preferred_element_type=F32)
        for h in range(hg):
            r0 = lo + h * head_dim
            d = decay_ref[b * heads + g * hg + h]
            nst_ref[0, r0:r0 + head_dim, :] = (st_ref[0, r0:r0 + head_dim, :] * d
                                               + outer[h * head_dim:(h + 1) * head_dim, :])
        c_row = jnp.broadcast_to(cg_ref[g, pl.ds(b, 1), :], (SUBLANES, n)).astype(BF16)
        y = lax.dot_general(c_row, nst_ref[0, lo:lo + gw, :].astype(BF16), (((1,), (1,)), ((), ())),
                            preferred_element_type=F32)[0:1, :]
        y = _group_rmsnorm_gate(y, xs_ref[pl.ds(b, 1), lo:lo + gw], z_ref[pl.ds(b, 1), lo:lo + gw],
                                dskip_ref[:, lo:lo + gw], normg_ref[:, lo:lo + gw])
        y_ref[0, :, lo:lo + gw] = y.astype(y_ref.dtype)


def _ssm_sample_state(decay, xdt_t, bg, cg, xs, z, dskip, normg, state, groups, head_dim):
    nb, d_ssm = xs.shape
    n = state.shape[-1]
    hg = d_ssm // groups // head_dim
    full2 = lambda b: (0, 0)
    full3 = lambda b: (0, 0, 0)
    return pl.pallas_call(
        functools.partial(_ssm_sample_state_body, groups=groups, hg=hg, head_dim=head_dim),
        grid=(nb,),
        in_specs=[pl.BlockSpec(memory_space=pltpu.SMEM),
                  pl.BlockSpec((d_ssm, nb), full2),
                  pl.BlockSpec((groups, nb, n), full3),
                  pl.BlockSpec((groups, nb, n), full3),
                  pl.BlockSpec((nb, d_ssm), full2),
                  pl.BlockSpec((nb, d_ssm), full2),
                  pl.BlockSpec((1, d_ssm), full2),
                  pl.BlockSpec((1, d_ssm), full2),
                  pl.BlockSpec((1, d_ssm, n), lambda b: (b, 0, 0))],
        out_specs=[pl.BlockSpec((1, 1, d_ssm), lambda b: (b, 0, 0)),
                   pl.BlockSpec((1, d_ssm, n), lambda b: (b, 0, 0))],
        out_shape=[jax.ShapeDtypeStruct((nb, 1, d_ssm), F32),
                   jax.ShapeDtypeStruct((nb, d_ssm, n), F32)],
        compiler_params=_params(("arbitrary",), 48 * 1024 * 1024),
        name="ssm_sample_state",
    )(decay, xdt_t, bg, cg, xs, z, dskip, normg, state)


def _softmax_rows(s):
    e = jnp.exp(s - jnp.max(s, axis=-1, keepdims=True))
    return e / jnp.sum(e, axis=-1, keepdims=True)


def _attn_prompt_body(q_ref, k_ref, v_ref, o_ref, *, scale):
    s = lax.dot_general(q_ref[...], k_ref[...].astype(BF16), (((1,), (1,)), ((), ())),
                        preferred_element_type=F32) * scale
    p = _softmax_rows(s)
    o_ref[...] = jnp.dot(p.astype(BF16), v_ref[...].astype(BF16), preferred_element_type=F32).astype(o_ref.dtype)


def _attn_prompt(q, k, v, batch, seqlen, n_mem, heads):
    m, d = q.shape
    hd = d // heads
    tq = min(512, seqlen)
    nq = seqlen // tq
    return pl.pallas_call(
        functools.partial(_attn_prompt_body, scale=hd ** -0.5),
        grid=(batch, heads, nq),
        in_specs=[pl.BlockSpec((tq, hd), lambda b, h, i: (b * nq + i, h)),
                  pl.BlockSpec((n_mem, hd), lambda b, h, i: (b, h)),
                  pl.BlockSpec((n_mem, hd), lambda b, h, i: (b, h))],
        out_specs=pl.BlockSpec((tq, hd), lambda b, h, i: (b * nq + i, h)),
        out_shape=jax.ShapeDtypeStruct((m, d), BF16),
        compiler_params=_params(("parallel", "parallel", "parallel"), 32 * 1024 * 1024),
        name="attn_prompt",
    )(q, k, v)


def _attn_sample_body(q_ref, k_ref, v_ref, o_ref, *, scale, chunk):
    q = q_ref[0]
    n_mem = k_ref.shape[2]
    s = jnp.concatenate(
        [jnp.sum(k_ref[0, 0, m0:m0 + chunk] * q[None], axis=-1, keepdims=True) for m0 in range(0, n_mem, chunk)],
        axis=0) * scale
    e = jnp.exp(s - jnp.max(s, axis=0, keepdims=True))
    p = e / jnp.sum(e, axis=0, keepdims=True)
    o = jnp.sum(p[0:chunk] * v_ref[0, 0, 0:chunk], axis=0)
    for m0 in range(chunk, n_mem, chunk):
        o = o + jnp.sum(p[m0:m0 + chunk] * v_ref[0, 0, m0:m0 + chunk], axis=0)
    o_ref[0] = o


def _attn_sample(q, k, v):
    _, nb, n_mem, heads, hd = k.shape
    chunk = min(32, n_mem)
    assert n_mem % chunk == 0
    kv_spec = pl.BlockSpec((1, 1, n_mem, heads, hd), lambda b: (0, b, 0, 0, 0))
    return pl.pallas_call(
        functools.partial(_attn_sample_body, scale=hd ** -0.5, chunk=chunk),
        grid=(nb,),
        in_specs=[pl.BlockSpec((1, heads, hd), lambda b: (b, 0, 0)), kv_spec, kv_spec],
        out_specs=pl.BlockSpec((1, heads, hd), lambda b: (b, 0, 0)),
        out_shape=jax.ShapeDtypeStruct((nb, heads, hd), F32),
        compiler_params=_params(("parallel",), 4 * n_mem * SUBLANES * hd * 4 + 8 * 1024 * 1024),
        name="attn_sample",
    )(q, k, v)


def kernel(x_prompt, x_sample, mem_prompt, state_conv, state_ssm_conv, state_ssm, cache_mem_k, cache_mem_v,
           g_mix, w_in, conv_w, conv_b, ln_g, ln_b, ssm_conv_w, ssm_conv_b, dt_bias, a_log, d_skip,
           ssm_norm_g, w_out, g_xattn, g_mem, w_q, w_k, w_v, w_o, g_mlp, w_up, w_down, g_final):
    depth = g_mix.shape[0]
    assert depth == 1
    batch, seqlen, d = x_prompt.shape
    nb = x_sample.shape[0]
    assert x_sample.shape[1] == 1
    d_conv = conv_w.shape[-1]
    c_xbc = ssm_conv_w.shape[-1]
    d_ssm = ssm_norm_g.shape[-1]
    heads = dt_bias.shape[-1]
    head_dim = d_ssm // heads
    n = state_ssm.shape[-1]
    groups = (c_xbc - d_ssm) // (2 * n)
    hg = heads // groups
    n_mem, xa_heads = cache_mem_k.shape[2], cache_mem_k.shape[3]
    cw = ssm_conv_w.shape[1]
    assert n == LANES and hg <= LANES

    o_z, o_x = 2 * d_conv, 2 * d_conv + d_ssm
    o_b, o_dt = o_x + d_ssm, o_x + c_xbc
    w_in_t = jnp.swapaxes(w_in, 1, 2)
    w_dt_t = jnp.pad(w_in_t[0, o_dt:].reshape(groups, hg, d), ((0, 0), (0, LANES - hg), (0, 0)))
    w_dt_t = w_dt_t.reshape(1, groups * LANES, d)
    wx, wbc = ssm_conv_w[0][:, :d_ssm], ssm_conv_w[0][:, d_ssm:]
    bx, bbc = ssm_conv_b[0][:d_ssm].reshape(1, d_ssm), ssm_conv_b[0][d_ssm:].reshape(1, c_xbc - d_ssm)
    dtb, alog = _pad_heads(dt_bias[0], groups), _pad_heads(a_log[0], groups)
    dskip = jnp.repeat(d_skip[0].astype(F32), head_dim).reshape(1, d_ssm)
    normg = ssm_norm_g[0].reshape(1, d_ssm)

    mp = batch * seqlen
    xp, xsm = x_prompt.reshape(mp, d), x_sample.reshape(nb, d)
    norm2 = lambda a_p, a_s, g, dtype: (_rmsnorm(a_p, g, dtype), _rmsnorm(a_s, g, dtype))
    in_proj = functools.partial(_proj, *norm2(xp, xsm, g_mix[0], BF16), out_dtype=F32, w_rows_are_outputs=True)
    glu, glu_s = in_proj(w_in_t, col0=0, ncols=o_z)
    z, z_s = in_proj(w_in_t, col0=o_z, ncols=d_ssm)
    xs, xs_s = in_proj(w_in_t, col0=o_x, ncols=d_ssm)
    bc, bc_s = in_proj(w_in_t, col0=o_b, ncols=o_dt - o_b)
    dt, dt_s = in_proj(w_dt_t)

    v_p, conv_hist = _conv_prompt(glu, conv_w[0], conv_b[0], ln_g[0], ln_b[0], batch, seqlen)
    y_p, st_p, w_out16 = _ssd_prompt(xs, z, bc, dt, wx, wbc, bx, bbc, dtb, alog, dskip, normg,
                                     batch, seqlen, groups, head_dim, n, cast_riders=(w_out[0],))
    keep = conv_w.shape[1] - 1
    new_conv_p = conv_hist[:, CONV_HALO - keep:, :]
    tail = lambda a: a.reshape(batch, seqlen, -1)[:, seqlen - (cw - 1):, :]
    new_ssm_conv_p = jnp.concatenate([tail(xs), tail(bc)], axis=-1)

    v_s, new_conv_t = _conv_sample(glu_s, jnp.swapaxes(state_conv[0], 0, 1), conv_w[0], conv_b[0], ln_g[0], ln_b[0])
    new_conv_s = jnp.swapaxes(new_conv_t, 0, 1)[None]
    hist = jnp.swapaxes(state_ssm_conv[0], 0, 1)
    xs_c, xdt, b_c, c_c, decay = _ssm_sample_pre(xs_s, bc_s, dt_s, hist, wx, wbc, bx, bbc, dtb, alog,
                                                  groups, head_dim, n)
    decay_flat = decay.reshape(nb, groups, LANES)[:, :, :hg].reshape(nb * heads)
    to_groups = lambda a: jnp.swapaxes(a.reshape(nb, groups, n), 0, 1)
    y_s, st_s = _ssm_sample_state(decay_flat, xdt.T, to_groups(b_c), to_groups(c_c), xs_c, z_s, dskip, normg,
                                  state_ssm[0].reshape(nb, d_ssm, n), groups, head_dim)
    new_ssm_conv_s = jnp.swapaxes(
        jnp.concatenate([hist[1:], jnp.concatenate([xs_s, bc_s], axis=-1)[None]], axis=0), 0, 1)

    x1, x1_s = _kgrid([v_p, y_p], [v_s, y_s.reshape(nb, d_ssm).astype(BF16)], w_out16,
                      res_p=xp, res_s=xsm)
    q, q_s = _proj(*norm2(x1, x1_s, g_xattn[0], BF16), w_q, out_dtype=BF16)
    mem_n = _rmsnorm(mem_prompt.reshape(batch * n_mem, d), g_mem[0], BF16)
    k_p, _ = _proj(mem_n, None, w_k, out_dtype=F32)
    v_mem_p, _ = _proj(mem_n, None, w_v, out_dtype=F32)
    o = _attn_prompt(q, k_p, v_mem_p, batch, seqlen, n_mem, xa_heads)
    o_s = _attn_sample(q_s.astype(F32).reshape(nb, xa_heads, d // xa_heads), cache_mem_k,
                       cache_mem_v).reshape(nb, d).astype(BF16)
    x2, x2_s = _proj(o, o_s, w_o, out_dtype=F32, res_p=x1, res_s=x1_s)
    up, up_s, w_down16 = _proj(*norm2(x2, x2_s, g_mlp[0], BF16), w_up, out_dtype=BF16, act="relu2",
                               cast_rider=w_down[0])
    x3, x3_s = _kgrid([up], [up_s], w_down16, res_p=x2, res_s=x2_s)
    y_prompt, y_sample = norm2(x3, x3_s, g_final, F32)

    kv_shape = (1, batch, n_mem, xa_heads, d // xa_heads)
    return (y_prompt.reshape(batch, seqlen, d), y_sample.reshape(nb, 1, d),
            new_conv_p[None], new_ssm_conv_p[None], st_p.reshape(1, batch, heads, head_dim, n),
            k_p.reshape(kv_shape), v_mem_p.reshape(kv_shape),
            new_conv_s, new_ssm_conv_s[None], st_s.reshape(1, nb, heads, head_dim, n))
```
